```python
import jax, jax.numpy as jnp
from jax import lax
import numpy as np

D_MODEL = 1024
BATCH = 8
SEQ = 8192
DEPTH = 2
DEC_BATCH = 8
DEC_SEQ = 16
PAST_LEN = 1024

CHUNK = 64
N_A_LAYERS = DEPTH // 2
N_B_LAYERS = DEPTH - N_A_LAYERS
CONV_WIDTH = 3
D_CONV = D_MODEL
N_HEADS = 16
HEAD_DIM = 64
D_ATTN = N_HEADS * HEAD_DIM
Q_BLOCK = 128
EPS = 1e-6

kernel_name = "yoco_shortconv_stickbreaking_step"


def _rmsnorm(x, g):
    xf = x.astype(jnp.float32)
    y = xf * lax.rsqrt(jnp.mean(xf * xf, axis=-1, keepdims=True) + EPS)
    return (y * g.astype(jnp.float32)).astype(x.dtype)


def _short_conv_mixer(xn, prev, w_in, conv_w, w_out):
    proj = xn @ w_in
    b_gate, c_gate, h, z = jnp.split(proj, 4, axis=-1)
    u = c_gate * h
    T = u.shape[1]
    up = jnp.concatenate([prev.astype(u.dtype), u], axis=1)
    conv = conv_w[0] * up[:, 0:T]
    for i in range(1, CONV_WIDTH):
        conv = conv + conv_w[i] * up[:, i:i + T]
    y = jax.nn.silu(z) * b_gate * conv
    return y @ w_out, up[:, T:]


def _sb_block(q, qpos, k, v, kpos):
    scale = HEAD_DIM ** -0.5
    z = jnp.einsum('bqhd,bkhd->bhqk', q, k, preferred_element_type=jnp.float32) * scale
    causal = kpos[None, :] < qpos[:, None]
    log_keep = jnp.where(causal, jax.nn.log_sigmoid(-z), 0.0)
    rev = lax.cumsum(log_keep, axis=log_keep.ndim - 1, reverse=True)
    after = jnp.concatenate([rev[..., 1:], jnp.zeros_like(rev[..., :1])], axis=-1)
    log_a = jnp.where(causal, jax.nn.log_sigmoid(z) + after, -jnp.inf)
    a = jnp.exp(log_a)
    o = jnp.einsum('bhqk,bkhd->bqhd', a, v.astype(jnp.float32))
    return o.astype(q.dtype)


def _sb_prompt(q, k, v):
    b, T, H, d = q.shape
    nb = T // Q_BLOCK
    pos = jnp.arange(T, dtype=jnp.int32)
    qb = q.reshape(b, nb, Q_BLOCK, H, d)
    qpos = pos.reshape(nb, Q_BLOCK)

    def per_seq(args):
        qs, ks, vs = args

        def per_block(a):
            qblk, qp = a
            return _sb_block(qblk[None], qp, ks[None], vs[None], pos)[0]

        return lax.map(per_block, (qs, qpos))

    out = lax.map(per_seq, (qb, k, v))
    return out.reshape(b, T, H, d)


def _sb_sample(q, k_new, v_new, past_k, past_v):
    past = past_k.shape[1]
    tn = q.shape[1]
    k_all = jnp.concatenate([past_k.astype(k_new.dtype), k_new], axis=1)
    v_all = jnp.concatenate([past_v.astype(v_new.dtype), v_new], axis=1)
    kpos = jnp.arange(past + tn, dtype=jnp.int32)
    qpos = past + jnp.arange(tn, dtype=jnp.int32)
    return _sb_block(q, qpos, k_all, v_all, kpos)


def _trunk(x, conv_prev, past_k, past_v,
           a_norm_pre, a_w_in, a_conv_w, a_w_out, a_norm_post,
           kv_norm, w_k, w_v,
           b_norm_pre, b_w_in, b_w_out, b_norm_post):
    b, T, _ = x.shape
    new_conv = []
    k = v = None
    for layer in range(DEPTH):
        if layer < N_A_LAYERS:
            i = layer
            h, st = _short_conv_mixer(_rmsnorm(x, a_norm_pre[i]), conv_prev[i],
                                      a_w_in[i], a_conv_w[i], a_w_out[i])
            new_conv.append(st)
            x = x + _rmsnorm(h, a_norm_post[i])
        else:
            j = layer - N_A_LAYERS
            if j == 0:
                kv_in = _rmsnorm(x, kv_norm)
                k = (kv_in @ w_k).reshape(b, T, N_HEADS, HEAD_DIM)
                v = (kv_in @ w_v).reshape(b, T, N_HEADS, HEAD_DIM)
            proj = _rmsnorm(x, b_norm_pre[j]) @ b_w_in[j]
            q, z = jnp.split(proj, 2, axis=-1)
            q = q.reshape(b, T, N_HEADS, HEAD_DIM)
            if past_k is None:
                o = _sb_prompt(q, k, v)
            else:
                o = _sb_sample(q, k, v, past_k, past_v)
            h = (jax.nn.silu(z) * o.reshape(b, T, D_ATTN)) @ b_w_out[j]
            x = x + _rmsnorm(h, b_norm_post[j])
    return x, jnp.stack(new_conv, axis=0), k, v


def setup_inputs(seed: int = 0) -> dict:
    key = jax.random.key(seed)
    ks = jax.random.split(key, 20)
    f32 = jnp.float32
    nrm = lambda k, shape, s: jax.random.normal(k, shape, f32) * s
    return {
        "x_prompt": nrm(ks[0], (BATCH, SEQ, D_MODEL), 1.0),
        "x_sample": nrm(ks[1], (DEC_BATCH, DEC_SEQ, D_MODEL), 1.0),
        "state_conv": nrm(ks[2], (N_A_LAYERS, DEC_BATCH, CONV_WIDTH - 1, D_CONV), 1.0),
        "cache_k": nrm(ks[3], (DEC_BATCH, PAST_LEN, N_HEADS, HEAD_DIM), 1.0),
        "cache_v": nrm(ks[4], (DEC_BATCH, PAST_LEN, N_HEADS, HEAD_DIM), 1.0),
        "a_norm_pre": 1.0 + nrm(ks[5], (N_A_LAYERS, D_MODEL), 0.02),
        "a_w_in": nrm(ks[6], (N_A_LAYERS, D_MODEL, 4 * D_CONV), D_MODEL ** -0.5),
        "a_conv_w": nrm(ks[7], (N_A_LAYERS, CONV_WIDTH, D_CONV), CONV_WIDTH ** -0.5),
        "a_w_out": nrm(ks[8], (N_A_LAYERS, D_CONV, D_MODEL), D_CONV ** -0.5),
        "a_norm_post": 1.0 + nrm(ks[9], (N_A_LAYERS, D_MODEL), 0.02),
        "kv_norm": 1.0 + nrm(ks[10], (D_MODEL,), 0.02),
        "w_k": nrm(ks[11], (D_MODEL, D_ATTN), D_MODEL ** -0.5),
        "w_v": nrm(ks[12], (D_MODEL, D_ATTN), D_MODEL ** -0.5),
        "b_norm_pre": 1.0 + nrm(ks[13], (N_B_LAYERS, D_MODEL), 0.02),
        "b_w_in": nrm(ks[14], (N_B_LAYERS, D_MODEL, 2 * D_ATTN), D_MODEL ** -0.5),
        "b_w_out": nrm(ks[15], (N_B_LAYERS, D_ATTN, D_MODEL), D_ATTN ** -0.5),
        "b_norm_post": 1.0 + nrm(ks[16], (N_B_LAYERS, D_MODEL), 0.02),
    }


def reference(x_prompt, x_sample, state_conv, cache_k, cache_v,
              a_norm_pre, a_w_in, a_conv_w, a_w_out, a_norm_post,
              kv_norm, w_k, w_v,
              b_norm_pre, b_w_in, b_w_out, b_norm_post):
    weights = (a_norm_pre, a_w_in, a_conv_w, a_w_out, a_norm_post,
               kv_norm, w_k, w_v, b_norm_pre, b_w_in, b_w_out, b_norm_post)
    conv0 = jnp.zeros((N_A_LAYERS, x_prompt.shape[0], CONV_WIDTH - 1, D_CONV), x_prompt.dtype)
    y_prompt, conv_state_prompt, k_prompt, v_prompt = _trunk(x_prompt, conv0, None, None, *weights)
    y_sample, conv_state_sample, k_sample, v_sample = _trunk(x_sample, state_conv, cache_k, cache_v, *weights)
    return (y_prompt, y_sample, conv_state_prompt, conv_state_sample, k_prompt, v_prompt, k_sample, v_sample)
```

```python
import functools
import math

import jax
import jax.numpy as jnp
from jax import lax
from jax.experimental import pallas as pl
from jax.experimental.pallas import tpu as pltpu

N_HEADS = 16
HEAD_DIM = 64
HEADS_PER_STEP = 2
PAIR = HEADS_PER_STEP * HEAD_DIM
EPS = 1e-6
CONV_WIDTH = 3
CONV_PAD = 8
Q_SCALE = (HEAD_DIM ** -0.5) * math.log2(math.e)

ROW_TILE = 256
Q_TILE = 512
K_TILE = 256
VMEM_LIMIT = 56 * 1024 * 1024

F32 = jnp.float32
BF16 = jnp.bfloat16


def _dot(a, b):
    return jnp.dot(a, b, preferred_element_type=F32)


def _dot_nt(a, b):
    return lax.dot_general(a, b, (((1,), (1,)), ((), ())), preferred_element_type=F32)


def _inv_rms(x):
    return lax.rsqrt(jnp.mean(x * x, axis=-1, keepdims=True) + EPS)


def _const_spec(shape):
    nd = len(shape)
    return pl.BlockSpec(shape, lambda *_: (0,) * nd, pipeline_mode=pl.Buffered(1))


def _trunk_a_kernel(x_ref, cs_ref, gpre_ref, win_ref, cw_ref, wout_ref, gpost_ref,
                    gkv_ref, wk_ref, wv_ref, gb_ref, bwin_ref,
                    x1_ref, cso_ref, k_ref, v_ref, q_ref, sz_ref, vb_ref, *rest, tm, emit_kt):
    if emit_kt:
        kt_ref, u_scr = rest
    else:
        (u_scr,) = rest
    d = x_ref.shape[-1]
    t = pl.program_id(1)

    @pl.when(t == 0)
    def _():
        u_scr[CONV_PAD - (CONV_WIDTH - 1):CONV_PAD, :] = cs_ref[0]

    x = x_ref[0]
    xn = (x * _inv_rms(x) * gpre_ref[...]).astype(BF16)
    c_gate = _dot(xn, win_ref[:, d:2 * d])
    h_in = _dot(xn, win_ref[:, 2 * d:3 * d])
    u = c_gate * h_in
    u_scr[CONV_PAD:CONV_PAD + tm, :] = u
    conv = cw_ref[CONV_WIDTH - 1:CONV_WIDTH, :] * u
    for i in range(CONV_WIDTH - 1):
        off = CONV_PAD - (CONV_WIDTH - 1) + i
        conv = conv + cw_ref[i:i + 1, :] * u_scr[off:off + tm, :]
    tail = u_scr[tm + CONV_PAD - (CONV_WIDTH - 1):tm + CONV_PAD, :]
    cso_ref[0] = tail
    u_scr[CONV_PAD - (CONV_WIDTH - 1):CONV_PAD, :] = tail

    b_gate = _dot(xn, win_ref[:, 0:d])
    z = _dot(xn, win_ref[:, 3 * d:4 * d])
    y = (z / (1.0 + jnp.exp(-z))) * b_gate * conv
    h = _dot(y.astype(BF16), wout_ref[...])
    x1 = x + h * _inv_rms(h) * gpost_ref[...]
    x1_ref[0] = x1

    xs = x1 * _inv_rms(x1)
    kv_in = (xs * gkv_ref[...]).astype(BF16)
    qn = (xs * gb_ref[...]).astype(BF16)
    k = _dot(kv_in, wk_ref[...])
    v = _dot(kv_in, wv_ref[...])
    k_ref[0] = k
    v_ref[0] = v
    vb_ref[0] = v.astype(BF16)
    if emit_kt:
        kt_ref[0] = k.T.astype(BF16)
    q = _dot(qn, bwin_ref[:, 0:d])
    q_ref[0] = (q * Q_SCALE).astype(BF16)
    zg = _dot(qn, bwin_ref[:, d:2 * d])
    sz_ref[0] = (zg / (1.0 + jnp.exp(-zg))).astype(BF16)


def _trunk_a(x, conv_state, gpre, win, cw, wout, gpost, gkv, wk, wv, gb, bwin, *, emit_kt):
    b, t, d = x.shape
    tm = ROW_TILE if t % ROW_TILE == 0 else t
    nt = t // tm
    row = lambda shape_last: pl.BlockSpec((1, tm, shape_last), lambda i, j: (i, j, 0))
    in_specs = [
        row(d),
        pl.BlockSpec((1, CONV_WIDTH - 1, d), lambda i, j: (i, 0, 0)),
        _const_spec((1, d)), _const_spec((d, 4 * d)), _const_spec((CONV_WIDTH, d)), _const_spec((d, d)),
        _const_spec((1, d)), _const_spec((1, d)), _const_spec((d, d)), _const_spec((d, d)),
        _const_spec((1, d)), _const_spec((d, 2 * d)),
    ]
    out_shape = [
        jax.ShapeDtypeStruct((b, t, d), F32),
        jax.ShapeDtypeStruct((b, CONV_WIDTH - 1, d), F32),
        jax.ShapeDtypeStruct((b, t, d), F32),
        jax.ShapeDtypeStruct((b, t, d), F32),
        jax.ShapeDtypeStruct((b, t, d), BF16),
        jax.ShapeDtypeStruct((b, t, d), BF16),
        jax.ShapeDtypeStruct((b, t, d), BF16),
    ]
    out_specs = [
        row(d),
        pl.BlockSpec((1, CONV_WIDTH - 1, d), lambda i, j: (i, 0, 0)),
        row(d), row(d), row(d), row(d), row(d),
    ]
    if emit_kt:
        out_shape.append(jax.ShapeDtypeStruct((b, d, t), BF16))
        out_specs.append(pl.BlockSpec((1, d, tm), lambda i, j: (i, 0, j)))
    return pl.pallas_call(
        functools.partial(_trunk_a_kernel, tm=tm, emit_kt=emit_kt),
        grid=(b, nt),
        in_specs=in_specs,
        out_specs=out_specs,
        out_shape=out_shape,
        scratch_shapes=[pltpu.VMEM((CONV_PAD + tm, d), F32)],
        compiler_params=pltpu.CompilerParams(
            dimension_semantics=("arbitrary", "arbitrary"), vmem_limit_bytes=VMEM_LIMIT),
        name="trunk_a_kt" if emit_kt else "trunk_a",
    )(x, conv_state, gpre, win, cw, wout, gpost, gkv, wk, wv, gb, bwin)


def _sb_tile(s, v, tri, c, causal):
    soft = jnp.log2(1.0 + jnp.exp2(-jnp.abs(s)))
    sp = jnp.maximum(s, 0.0) + soft
    if causal is not None:
        sp = jnp.where(causal, sp, 0.0)
    later = _dot(sp.astype(BF16), tri)
    p = jnp.exp2(s - (sp + later + c))
    if causal is not None:
        p = jnp.where(causal, p, 0.0)
    return _dot(p.astype(BF16), v), later[:, :1] + sp[:, :1]


def _split_heads(q2):
    lane = lax.broadcasted_iota(jnp.int32, q2.shape, 1)
    zero = jnp.zeros_like(q2)
    return jnp.where(lane < HEAD_DIM, q2, zero), jnp.where(lane >= HEAD_DIM, q2, zero)


def _merge_heads(o0, o1):
    lane = lax.broadcasted_iota(jnp.int32, o0.shape, 1)
    return jnp.where(lane < HEAD_DIM, o0, o1)


def _attn_prompt_kernel(q_ref, kt_ref, v_ref, tri_ref, o_ref, acc_ref, c_ref, *, tq, tk):
    i = pl.program_id(2)
    qh = _split_heads(q_ref[0])
    tri = tri_ref[...]
    acc_ref[...] = jnp.zeros_like(acc_ref)
    c_ref[...] = jnp.zeros_like(c_ref)
    ndiag = tq // tk
    row = lax.broadcasted_iota(jnp.int32, (tq, tk), 0)
    col = lax.broadcasted_iota(jnp.int32, (tq, tk), 1)

    def step(kb, diag_offset):
        ks = pl.multiple_of(kb * tk, tk)
        kt = kt_ref[0, :, pl.ds(ks, tk)]
        v = v_ref[0, pl.ds(ks, tk), :]
        causal = None if diag_offset is None else (col + diag_offset * tk) < row
        for h in range(HEADS_PER_STEP):
            o_c, c_add = _sb_tile(_dot(qh[h], kt), v, tri, c_ref[h], causal)
            acc_ref[h] += o_c
            c_ref[h] += c_add

    for dblk in reversed(range(ndiag)):
        step(i * ndiag + dblk, dblk)

    def body(j, carry):
        step(i * ndiag - 1 - j, None)
        return carry

    lax.fori_loop(0, i * ndiag, body, 0)
    o_ref[0] = _merge_heads(acc_ref[0], acc_ref[1]).astype(o_ref.dtype)


def _tri_matrix(n):
    j = lax.broadcasted_iota(jnp.int32, (n, n), 0)
    k = lax.broadcasted_iota(jnp.int32, (n, n), 1)
    return (j > k).astype(BF16)


def _attn_prompt(q, kt, vb):
    b, t, d = q.shape
    tq = min(Q_TILE, t)
    tk = min(K_TILE, tq)
    return pl.pallas_call(
        functools.partial(_attn_prompt_kernel, tq=tq, tk=tk),
        grid=(b, d // PAIR, t // tq),
        in_specs=[
            pl.BlockSpec((1, tq, PAIR), lambda bi, hp, qi: (bi, qi, hp)),
            pl.BlockSpec((1, PAIR, t), lambda bi, hp, qi: (bi, hp, 0)),
            pl.BlockSpec((1, t, PAIR), lambda bi, hp, qi: (bi, 0, hp)),
            _const_spec((tk, tk)),
        ],
        out_specs=pl.BlockSpec((1, tq, PAIR), lambda bi, hp, qi: (bi, qi, hp)),
        out_shape=jax.ShapeDtypeStruct((b, t, d), BF16),
        scratch_shapes=[pltpu.VMEM((HEADS_PER_STEP, tq, PAIR), F32),
                        pltpu.VMEM((HEADS_PER_STEP, tq, 1), F32)],
        compiler_params=pltpu.CompilerParams(
            dimension_semantics=("arbitrary", "arbitrary", "arbitrary"), vmem_limit_bytes=VMEM_LIMIT),
        name="attn_prompt",
    )(q, kt, vb, _tri_matrix(tk))


def _attn_sample_kernel(q_ref, kn_ref, vn_ref, pk_ref, pv_ref, tri_ref, o_ref, *, tk, new_pad):
    qh = _split_heads(q_ref[0])
    tn = q_ref.shape[1]
    past = pk_ref.shape[1]
    tri = tri_ref[...]
    pad = jnp.zeros((new_pad - tn, PAIR), BF16)
    kn = jnp.concatenate([kn_ref[0].astype(BF16), pad], axis=0)
    vn = jnp.concatenate([vn_ref[0], pad], axis=0)
    row = lax.broadcasted_iota(jnp.int32, (tn, new_pad), 0)
    col = lax.broadcasted_iota(jnp.int32, (tn, new_pad), 1)
    causal = col < row
    outs = []
    for h in range(HEADS_PER_STEP):
        c = jnp.zeros((tn, 1), F32)
        o, c_add = _sb_tile(_dot_nt(qh[h], kn), vn, tri[:new_pad, :new_pad], c, causal)
        c = c + c_add
        for kb in reversed(range(past // tk)):
            kp = pk_ref[0, kb * tk:(kb + 1) * tk, :].astype(BF16)
            vp = pv_ref[0, kb * tk:(kb + 1) * tk, :].astype(BF16)
            o_c, c_add = _sb_tile(_dot_nt(qh[h], kp), vp, tri, c, None)
            o = o + o_c
            c = c + c_add
        outs.append(o)
    o_ref[0] = _merge_heads(outs[0], outs[1]).astype(o_ref.dtype)


def _attn_sample(q, k_new, vb_new, past_k, past_v):
    b, tn, d = q.shape
    past = past_k.shape[1]
    tk = K_TILE
    new_pad = 128
    assert past % tk == 0 and tn <= new_pad and tn % 16 == 0
    blk = lambda rows: pl.BlockSpec((1, rows, PAIR), lambda bi, hp: (bi, 0, hp))
    return pl.pallas_call(
        functools.partial(_attn_sample_kernel, tk=tk, new_pad=new_pad),
        grid=(b, d // PAIR),
        in_specs=[blk(tn), blk(tn), blk(tn), blk(past), blk(past), _const_spec((tk, tk))],
        out_specs=blk(tn),
        out_shape=jax.ShapeDtypeStruct((b, tn, d), BF16),
        compiler_params=pltpu.CompilerParams(
            dimension_semantics=("arbitrary", "arbitrary"), vmem_limit_bytes=VMEM_LIMIT),
        name="attn_sample",
    )(q, k_new, vb_new, past_k.reshape(b, past, d), past_v.reshape(b, past, d), _tri_matrix(tk))


def _out_kernel(o_ref, sz_ref, x1_ref, w_ref, g_ref, y_ref):
    gated = (o_ref[0].astype(F32) * sz_ref[0].astype(F32)).astype(BF16)
    h = _dot(gated, w_ref[...])
    y_ref[0] = x1_ref[0] + h * _inv_rms(h) * g_ref[...]


def _out_proj(o, sz, x1, w, g):
    b, t, d = x1.shape
    tm = ROW_TILE if t % ROW_TILE == 0 else t
    row = pl.BlockSpec((1, tm, d), lambda i, j: (i, j, 0))
    return pl.pallas_call(
        _out_kernel,
        grid=(b, t // tm),
        in_specs=[row, row, row, _const_spec((d, d)), _const_spec((1, d))],
        out_specs=row,
        out_shape=jax.ShapeDtypeStruct((b, t, d), F32),
        compiler_params=pltpu.CompilerParams(
            dimension_semantics=("arbitrary", "arbitrary"), vmem_limit_bytes=VMEM_LIMIT),
        name="out_proj",
    )(o, sz, x1, w, g)


def _trunk(x, conv_state, past_k, past_v, w):
    b, t, d = x.shape
    outs = _trunk_a(x, conv_state, *w["a"], emit_kt=past_k is None)
    x1, cso, k, v, q, sz, vb = outs[:7]
    if past_k is None:
        o = _attn_prompt(q, outs[7], vb)
    else:
        o = _attn_sample(q, k, vb, past_k, past_v)
    y = _out_proj(o, sz, x1, *w["out"])
    return (y, cso[None], k.reshape(b, t, N_HEADS, HEAD_DIM), v.reshape(b, t, N_HEADS, HEAD_DIM))


def kernel(x_prompt, x_sample, state_conv, cache_k, cache_v, a_norm_pre, a_w_in, a_conv_w, a_w_out, a_norm_post,
           kv_norm, w_k, w_v, b_norm_pre, b_w_in, b_w_out, b_norm_post):
    assert a_w_in.shape[0] == 1 and b_w_in.shape[0] == 1, "one conv layer followed by one attention layer"
    d = x_prompt.shape[-1]
    assert d == N_HEADS * HEAD_DIM
    vec = lambda g: g.reshape(1, d).astype(F32)
    w = {
        "a": (vec(a_norm_pre[0]), a_w_in[0].astype(BF16), a_conv_w[0].astype(F32), a_w_out[0].astype(BF16),
              vec(a_norm_post[0]), vec(kv_norm), w_k.astype(BF16), w_v.astype(BF16),
              vec(b_norm_pre[0]), b_w_in[0].astype(BF16)),
        "out": (b_w_out[0].astype(BF16), vec(b_norm_post[0])),
    }
    conv0 = jnp.zeros((x_prompt.shape[0], CONV_WIDTH - 1, d), x_prompt.dtype)
    y_p, cs_p, k_p, v_p = _trunk(x_prompt, conv0, None, None, w)
    y_s, cs_s, k_s, v_s = _trunk(x_sample, state_conv[0], cache_k, cache_v, w)
    return (y_p, y_s, cs_p, cs_s, k_p, v_p, k_s, v_s)
```

```python
import functools
import math

import jax
import jax.numpy as jnp
from jax import lax
from jax.experimental import pallas as pl
from jax.experimental.pallas import tpu as pltpu

N_HEADS = 16
HEAD_DIM = 64
HEADS_PER_STEP = 2
PAIR = HEADS_PER_STEP * HEAD_DIM
LANES = 128
EPS = 1e-6
CONV_WIDTH = 3
CONV_PAD = 8
LOG2E = math.log2(math.e)
Q_SCALE = (HEAD_DIM ** -0.5) * LOG2E

ROW_TILE = 512
Q_TILE = 1024
K_TILE = 256
K_BLOCK = 1024
NEW_KEY_PAD = 128
VMEM_LIMIT = 56 * 1024 * 1024

F32 = jnp.float32
BF16 = jnp.bfloat16


def _dot(a, b):
    return jnp.dot(a, b, preferred_element_type=F32)


def _dot_nt(a, b):
    return lax.dot_general(a, b, (((1,), (1,)), ((), ())), preferred_element_type=F32)


def _inv_rms(x):
    return lax.rsqrt(jnp.mean(x * x, axis=-1, keepdims=True) + EPS)


def _const_spec(shape):
    nd = len(shape)
    return pl.BlockSpec(shape, lambda *_: (0,) * nd, pipeline_mode=pl.Buffered(1))


def _trunk_a_kernel(x_ref, cs_ref, gpre_ref, win_ref, cw_ref, wout_ref, gpost_ref,
                    gkv_ref, wk_ref, wv_ref, gb_ref, bwin_ref,
                    x1_ref, cso_ref, k_ref, v_ref, q_ref, sz_ref, vb_ref, *rest, tm, emit_kt):
    if emit_kt:
        kt_ref, u_scr = rest
    else:
        (u_scr,) = rest
    d = x_ref.shape[-1]
    t = pl.program_id(1)

    @pl.when(t == 0)
    def _():
        u_scr[CONV_PAD - (CONV_WIDTH - 1):CONV_PAD, :] = cs_ref[0]

    x = x_ref[0]
    xn = (x * _inv_rms(x) * gpre_ref[...]).astype(BF16)
    c_gate = _dot(xn, win_ref[:, d:2 * d])
    h_in = _dot(xn, win_ref[:, 2 * d:3 * d])
    u = c_gate * h_in
    u_scr[CONV_PAD:CONV_PAD + tm, :] = u
    conv = cw_ref[CONV_WIDTH - 1:CONV_WIDTH, :] * u
    for i in range(CONV_WIDTH - 1):
        off = CONV_PAD - (CONV_WIDTH - 1) + i
        conv = conv + cw_ref[i:i + 1, :] * u_scr[off:off + tm, :]
    tail = u_scr[tm + CONV_PAD - (CONV_WIDTH - 1):tm + CONV_PAD, :]
    cso_ref[0] = tail
    u_scr[CONV_PAD - (CONV_WIDTH - 1):CONV_PAD, :] = tail

    b_gate = _dot(xn, win_ref[:, 0:d])
    z = _dot(xn, win_ref[:, 3 * d:4 * d])
    y = (z / (1.0 + jnp.exp(-z))) * b_gate * conv
    h = _dot(y.astype(BF16), wout_ref[...])
    x1 = x + h * _inv_rms(h) * gpost_ref[...]
    x1_ref[0] = x1

    xs = x1 * _inv_rms(x1)
    kv_in = (xs * gkv_ref[...]).astype(BF16)
    qn = (xs * gb_ref[...]).astype(BF16)
    k = _dot(kv_in, wk_ref[...])
    v = _dot(kv_in, wv_ref[...])
    k_ref[0] = k
    v_ref[0] = v
    vb_ref[0] = v.astype(BF16)
    if emit_kt:
        kt_ref[0] = k.T.astype(BF16)
    q = _dot(qn, bwin_ref[:, 0:d])
    q_ref[0] = (q * Q_SCALE).astype(BF16)
    zg = _dot(qn, bwin_ref[:, d:2 * d])
    sz_ref[0] = (zg / (1.0 + jnp.exp(-zg))).astype(BF16)


def _trunk_a(x, conv_state, gpre, win, cw, wout, gpost, gkv, wk, wv, gb, bwin, *, emit_kt):
    b, t, d = x.shape
    tm = ROW_TILE if t % ROW_TILE == 0 else t
    nt = t // tm
    row = lambda shape_last: pl.BlockSpec((1, tm, shape_last), lambda i, j: (i, j, 0))
    in_specs = [
        row(d),
        pl.BlockSpec((1, CONV_WIDTH - 1, d), lambda i, j: (i, 0, 0)),
        _const_spec((1, d)), _const_spec((d, 4 * d)), _const_spec((CONV_WIDTH, d)), _const_spec((d, d)),
        _const_spec((1, d)), _const_spec((1, d)), _const_spec((d, d)), _const_spec((d, d)),
        _const_spec((1, d)), _const_spec((d, 2 * d)),
    ]
    out_shape = [
        jax.ShapeDtypeStruct((b, t, d), F32),
        jax.ShapeDtypeStruct((b, CONV_WIDTH - 1, d), F32),
        jax.ShapeDtypeStruct((b, t, d), F32),
        jax.ShapeDtypeStruct((b, t, d), F32),
        jax.ShapeDtypeStruct((b, t, d), BF16),
        jax.ShapeDtypeStruct((b, t, d), BF16),
        jax.ShapeDtypeStruct((b, t, d), BF16),
    ]
    out_specs = [
        row(d),
        pl.BlockSpec((1, CONV_WIDTH - 1, d), lambda i, j: (i, 0, 0)),
        row(d), row(d), row(d), row(d), row(d),
    ]
    if emit_kt:
        out_shape.append(jax.ShapeDtypeStruct((b, d, t), BF16))
        out_specs.append(pl.BlockSpec((1, d, tm), lambda i, j: (i, 0, j)))
    return pl.pallas_call(
        functools.partial(_trunk_a_kernel, tm=tm, emit_kt=emit_kt),
        grid=(b, nt),
        in_specs=in_specs,
        out_specs=out_specs,
        out_shape=out_shape,
        scratch_shapes=[pltpu.VMEM((CONV_PAD + tm, d), F32)],
        compiler_params=pltpu.CompilerParams(
            dimension_semantics=("arbitrary", "arbitrary"), vmem_limit_bytes=VMEM_LIMIT),
        name="trunk_a_kt" if emit_kt else "trunk_a",
    )(x, conv_state, gpre, win, cw, wout, gpost, gkv, wk, wv, gb, bwin)


def _neg_abs(x):
    bits = lax.bitcast_convert_type(x, jnp.uint16) | jnp.uint16(0x8000)
    return lax.bitcast_convert_type(bits, BF16)


def _sb_block(s_chunk, v, c, tri, causal):
    chunk = tri.shape[0]
    last_lane = lax.broadcasted_iota(jnp.int32, c.shape, 1) == LANES - 1
    ps = []
    for j in reversed(range(v.shape[0] // chunk)):
        sb = s_chunk(j).astype(BF16)
        e = jnp.exp2(_neg_abs(sb))
        t = 1.0 + e
        r = e - (t - 1.0)
        sp_j = jnp.maximum(sb, jnp.zeros_like(sb)) + (jnp.log(t) + r) * LOG2E
        if causal is not None:
            sp_j = jnp.where(causal, sp_j, jnp.zeros_like(sp_j))
        later = _dot(sp_j, tri)
        tail = later[:, chunk - LANES:]
        parts = [later[:, t * LANES:(t + 1) * LANES] + c for t in range(chunk // LANES - 1)]
        parts.append(jnp.where(last_lane, c, tail + c))
        behind = jnp.concatenate(parts, axis=1).astype(BF16)
        p = jnp.exp2(sb - sp_j - behind)
        if causal is not None:
            p = jnp.where(causal, p, jnp.zeros_like(p))
        ps.append(p)
        c = c + jnp.broadcast_to(tail[:, LANES - 1:], c.shape)
    return _dot(jnp.concatenate(ps[::-1], axis=1), v), c


def _split_heads(q2):
    lane = lax.broadcasted_iota(jnp.int32, q2.shape, 1)
    zero = jnp.zeros_like(q2)
    return jnp.where(lane < HEAD_DIM, q2, zero), jnp.where(lane >= HEAD_DIM, q2, zero)


def _merge_heads(o0, o1):
    lane = lax.broadcasted_iota(jnp.int32, o0.shape, 1)
    return jnp.where(lane < HEAD_DIM, o0, o1)


def _attn_prompt_kernel(q_ref, kt_ref, v_ref, tri_ref, o_ref, q_scr, acc_ref, c_ref, *, tq, tk, kb):
    i = pl.program_id(2)
    tri = tri_ref[...]
    n_chunks = tq // tk
    for r in range(n_chunks):
        heads = _split_heads(q_ref[0, r * tk:(r + 1) * tk, :])
        for h in range(HEADS_PER_STEP):
            q_scr[(HEADS_PER_STEP * r + h) * tk:(HEADS_PER_STEP * r + h + 1) * tk, :] = heads[h]
    acc_ref[...] = jnp.zeros_like(acc_ref)
    c_ref[...] = jnp.zeros_like(c_ref)

    def block(key_start, n_keys, row0, causal):
        kt = kt_ref[0, :, pl.ds(key_start, n_keys)]
        v = v_ref[0, pl.ds(key_start, n_keys), :]
        q_rows = q_scr[row0:, :]
        o, c_new = _sb_block(lambda j: _dot(q_rows, kt[:, j * tk:(j + 1) * tk]), v,
                             c_ref[row0:, :], tri, causal)
        acc_ref[row0:, :] += o
        c_ref[row0:, :] = c_new

    q0 = pl.multiple_of(i * tq, tq)
    for d in reversed(range(n_chunks)):
        rows = HEADS_PER_STEP * (tq - d * tk)
        row = lax.broadcasted_iota(jnp.int32, (rows, tk), 0)
        col = lax.broadcasted_iota(jnp.int32, (rows, tk), 1)
        causal = col < jnp.where(row < HEADS_PER_STEP * tk, row & (tk - 1), tk)
        block(q0 + d * tk, tk, HEADS_PER_STEP * d * tk, causal)

    def body(j, carry):
        block(pl.multiple_of(q0 - (j + 1) * kb, kb), kb, 0, None)
        return carry

    lax.fori_loop(0, i * (tq // kb), body, 0)
    for r in range(n_chunks):
        base = HEADS_PER_STEP * r * tk
        o_ref[0, r * tk:(r + 1) * tk, :] = _merge_heads(
            acc_ref[base:base + tk, :], acc_ref[base + tk:base + 2 * tk, :]).astype(o_ref.dtype)


def _tri_matrix(n):
    j = lax.broadcasted_iota(jnp.int32, (n, n), 0)
    k = lax.broadcasted_iota(jnp.int32, (n, n), 1)
    return ((j > k) | (k == n - 1)).astype(BF16)


def _attn_prompt(q, kt, vb):
    b, t, d = q.shape
    tq = min(Q_TILE, t)
    tk = min(K_TILE, tq)
    kb = min(K_BLOCK, tq)
    assert t % tq == 0 and tq % tk == 0 and tq % kb == 0 and kb % tk == 0
    return pl.pallas_call(
        functools.partial(_attn_prompt_kernel, tq=tq, tk=tk, kb=kb),
        grid=(b, d // PAIR, t // tq),
        in_specs=[
            pl.BlockSpec((1, tq, PAIR), lambda bi, hp, qi: (bi, qi, hp)),
            pl.BlockSpec((1, PAIR, t), lambda bi, hp, qi: (bi, hp, 0)),
            pl.BlockSpec((1, t, PAIR), lambda bi, hp, qi: (bi, 0, hp)),
            _const_spec((tk, tk)),
        ],
        out_specs=pl.BlockSpec((1, tq, PAIR), lambda bi, hp, qi: (bi, qi, hp)),
        out_shape=jax.ShapeDtypeStruct((b, t, d), BF16),
        scratch_shapes=[pltpu.VMEM((HEADS_PER_STEP * tq, PAIR), BF16),
                        pltpu.VMEM((HEADS_PER_STEP * tq, PAIR), F32),
                        pltpu.VMEM((HEADS_PER_STEP * tq, LANES), F32)],
        compiler_params=pltpu.CompilerParams(
            dimension_semantics=("arbitrary", "arbitrary", "arbitrary"), vmem_limit_bytes=VMEM_LIMIT),
        name="attn_prompt",
    )(q, kt, vb, _tri_matrix(tk))


def _attn_sample_kernel(q_ref, kn_ref, vn_ref, pk_ref, pv_ref, tri_new_ref, tri_ref, o_ref):
    tn = q_ref.shape[1]
    q_rows = jnp.concatenate(_split_heads(q_ref[0]), axis=0)
    pad = jnp.zeros((NEW_KEY_PAD - tn, PAIR), BF16)
    kn = jnp.concatenate([kn_ref[0].astype(BF16), pad], axis=0)
    vn = jnp.concatenate([vn_ref[0], pad], axis=0)
    row = lax.broadcasted_iota(jnp.int32, (HEADS_PER_STEP * tn, NEW_KEY_PAD), 0)
    col = lax.broadcasted_iota(jnp.int32, (HEADS_PER_STEP * tn, NEW_KEY_PAD), 1)
    causal = col < jnp.where(row < tn, row, row - tn)
    c = jnp.zeros((HEADS_PER_STEP * tn, LANES), F32)
    o_new, c = _sb_block(lambda j: _dot_nt(q_rows, kn), vn, c, tri_new_ref[...], causal)
    kp = pk_ref[0].astype(BF16)
    o_past, _ = _sb_block(lambda j: _dot_nt(q_rows, kp[j * K_TILE:(j + 1) * K_TILE]), pv_ref[0].astype(BF16),
                          c, tri_ref[...], None)
    o = o_new + o_past
    o_ref[0] = _merge_heads(o[:tn], o[tn:]).astype(o_ref.dtype)


def _attn_sample(q, k_new, vb_new, past_k, past_v):
    b, tn, d = q.shape
    past = past_k.shape[1]
    assert HEADS_PER_STEP == 2 and past % K_TILE == 0 and tn <= NEW_KEY_PAD and tn % 16 == 0
    blk = lambda rows: pl.BlockSpec((1, rows, PAIR), lambda bi, hp: (bi, 0, hp))
    return pl.pallas_call(
        _attn_sample_kernel,
        grid=(b, d // PAIR),
        in_specs=[blk(tn), blk(tn), blk(tn), blk(past), blk(past),
                  _const_spec((NEW_KEY_PAD, NEW_KEY_PAD)), _const_spec((K_TILE, K_TILE))],
        out_specs=blk(tn),
        out_shape=jax.ShapeDtypeStruct((b, tn, d), BF16),
        compiler_params=pltpu.CompilerParams(
            dimension_semantics=("arbitrary", "arbitrary"), vmem_limit_bytes=VMEM_LIMIT),
        name="attn_sample",
    )(q, k_new, vb_new, past_k.reshape(b, past, d), past_v.reshape(b, past, d),
      _tri_matrix(NEW_KEY_PAD), _tri_matrix(K_TILE))


def _out_kernel(o_ref, sz_ref, x1_ref, w_ref, g_ref, y_ref):
    gated = (o_ref[0].astype(F32) * sz_ref[0].astype(F32)).astype(BF16)
    h = _dot(gated, w_ref[...])
    y_ref[0] = x1_ref[0] + h * _inv_rms(h) * g_ref[...]


def _out_proj(o, sz, x1, w, g):
    b, t, d = x1.shape
    tm = ROW_TILE if t % ROW_TILE == 0 else t
    row = pl.BlockSpec((1, tm, d), lambda i, j: (i, j, 0))
    return pl.pallas_call(
        _out_kernel,
        grid=(b, t // tm),
        in_specs=[row, row, row, _const_spec((d, d)), _const_spec((1, d))],
        out_specs=row,
        out_shape=jax.ShapeDtypeStruct((b, t, d), F32),
        compiler_params=pltpu.CompilerParams(
            dimension_semantics=("arbitrary", "arbitrary"), vmem_limit_bytes=VMEM_LIMIT),
        name="out_proj",
    )(o, sz, x1, w, g)


def _trunk(x, conv_state, past_k, past_v, w):
    b, t, d = x.shape
    outs = _trunk_a(x, conv_state, *w["a"], emit_kt=past_k is None)
    x1, cso, k, v, q, sz, vb = outs[:7]
    if past_k is None:
        o = _attn_prompt(q, outs[7], vb)
    else:
        o = _attn_sample(q, k, vb, past_k, past_v)
    y = _out_proj(o, sz, x1, *w["out"])
    return (y, cso[None], k.reshape(b, t, N_HEADS, HEAD_DIM), v.reshape(b, t, N_HEADS, HEAD_DIM))


def kernel(x_prompt, x_sample, state_conv, cache_k, cache_v, a_norm_pre, a_w_in, a_conv_w, a_w_out, a_norm_post,
           kv_norm, w_k, w_v, b_norm_pre, b_w_in, b_w_out, b_norm_post):
    assert a_w_in.shape[0] == 1 and b_w_in.shape[0] == 1, "one conv layer followed by one attention layer"
    d = x_prompt.shape[-1]
    assert d == N_HEADS * HEAD_DIM
    vec = lambda g: g.reshape(1, d).astype(F32)
    w = {
        "a": (vec(a_norm_pre[0]), a_w_in[0].astype(BF16), a_conv_w[0].astype(F32), a_w_out[0].astype(BF16),
              vec(a_norm_post[0]), vec(kv_norm), w_k.astype(BF16), w_v.astype(BF16),
              vec(b_norm_pre[0]), b_w_in[0].astype(BF16)),
        "out": (b_w_out[0].astype(BF16), vec(b_norm_post[0])),
    }
    conv0 = jnp.zeros((x_prompt.shape[0], CONV_WIDTH - 1, d), x_prompt.dtype)
    y_p, cs_p, k_p, v_p = _trunk(x_prompt, conv0, None, None, w)
    y_s, cs_s, k_s, v_s = _trunk(x_sample, state_conv[0], cache_k, cache_v, w)
    return (y_p, y_s, cs_p, cs_s, k_p, v_p, k_s, v_s)
```

```python
import functools
import math

import jax
import jax.numpy as jnp
from jax import lax
from jax.experimental import pallas as pl
from jax.experimental.pallas import tpu as pltpu

N_HEADS = 16
HEAD_DIM = 64
HEADS_PER_STEP = 2
PAIR = HEADS_PER_STEP * HEAD_DIM
LANES = 128
EPS = 1e-6
CONV_WIDTH = 3
CONV_PAD = 8
LOG2E = math.log2(math.e)
Q_SCALE = (HEAD_DIM ** -0.5) * LOG2E

ROW_TILE = 512
Q_TILE = 1024
K_TILE = 256
K_BLOCK = 1024
NEW_KEY_PAD = 128
VMEM_LIMIT = 56 * 1024 * 1024

F32 = jnp.float32
BF16 = jnp.bfloat16


def _dot(a, b):
    return jnp.dot(a, b, preferred_element_type=F32)


def _dot_nt(a, b):
    return lax.dot_general(a, b, (((1,), (1,)), ((), ())), preferred_element_type=F32)


def _inv_rms(x):
    return lax.rsqrt(jnp.mean(x * x, axis=-1, keepdims=True) + EPS)


def _const_spec(shape):
    nd = len(shape)
    return pl.BlockSpec(shape, lambda *_: (0,) * nd, pipeline_mode=pl.Buffered(1))


def _trunk_a_kernel(x_ref, cs_ref, gpre_ref, win_ref, cw_ref, wout_ref, gpost_ref,
                    gkv_ref, wk_ref, wv_ref, gb_ref, bwin_ref,
                    x1_ref, cso_ref, k_ref, v_ref, q_ref, sz_ref, *rest, tm, transposed_kv):
    if transposed_kv:
        kt_ref, vt_ref, u_scr = rest
    else:
        vb_ref, u_scr = rest
    d = x_ref.shape[-1]
    t = pl.program_id(1)

    @pl.when(t == 0)
    def _():
        u_scr[CONV_PAD - (CONV_WIDTH - 1):CONV_PAD, :] = cs_ref[0]

    x = x_ref[0]
    xn = (x * _inv_rms(x) * gpre_ref[...]).astype(BF16)
    c_gate = _dot(xn, win_ref[:, d:2 * d])
    h_in = _dot(xn, win_ref[:, 2 * d:3 * d])
    u = c_gate * h_in
    u_scr[CONV_PAD:CONV_PAD + tm, :] = u
    conv = cw_ref[CONV_WIDTH - 1:CONV_WIDTH, :] * u
    for i in range(CONV_WIDTH - 1):
        off = CONV_PAD - (CONV_WIDTH - 1) + i
        conv = conv + cw_ref[i:i + 1, :] * u_scr[off:off + tm, :]
    tail = u_scr[tm + CONV_PAD - (CONV_WIDTH - 1):tm + CONV_PAD, :]
    cso_ref[0] = tail
    u_scr[CONV_PAD - (CONV_WIDTH - 1):CONV_PAD, :] = tail

    b_gate = _dot(xn, win_ref[:, 0:d])
    z = _dot(xn, win_ref[:, 3 * d:4 * d])
    y = (z / (1.0 + jnp.exp(-z))) * b_gate * conv
    h = _dot(y.astype(BF16), wout_ref[...])
    x1 = x + h * _inv_rms(h) * gpost_ref[...]
    x1_ref[0] = x1

    xs = x1 * _inv_rms(x1)
    kv_in = (xs * gkv_ref[...]).astype(BF16)
    qn = (xs * gb_ref[...]).astype(BF16)
    k = _dot(kv_in, wk_ref[...])
    v = _dot(kv_in, wv_ref[...])
    k_ref[0] = k
    v_ref[0] = v
    if transposed_kv:
        kt_ref[0] = k.T.astype(BF16)
        vt_ref[0] = v.T.astype(BF16)
    else:
        vb_ref[0] = v.astype(BF16)
    q = _dot(qn, bwin_ref[:, 0:d])
    q_ref[0] = (q * Q_SCALE).astype(BF16)
    zg = _dot(qn, bwin_ref[:, d:2 * d])
    sz_ref[0] = (zg / (1.0 + jnp.exp(-zg))).astype(BF16)


def _trunk_a(x, conv_state, gpre, win, cw, wout, gpost, gkv, wk, wv, gb, bwin, *, transposed_kv):
    b, t, d = x.shape
    tm = ROW_TILE if t % ROW_TILE == 0 else t
    nt = t // tm
    row = lambda shape_last: pl.BlockSpec((1, tm, shape_last), lambda i, j: (i, j, 0))
    in_specs = [
        row(d),
        pl.BlockSpec((1, CONV_WIDTH - 1, d), lambda i, j: (i, 0, 0)),
        _const_spec((1, d)), _const_spec((d, 4 * d)), _const_spec((CONV_WIDTH, d)), _const_spec((d, d)),
        _const_spec((1, d)), _const_spec((1, d)), _const_spec((d, d)), _const_spec((d, d)),
        _const_spec((1, d)), _const_spec((d, 2 * d)),
    ]
    out_shape = [
        jax.ShapeDtypeStruct((b, t, d), F32),
        jax.ShapeDtypeStruct((b, CONV_WIDTH - 1, d), F32),
        jax.ShapeDtypeStruct((b, t, d), F32),
        jax.ShapeDtypeStruct((b, t, d), F32),
        jax.ShapeDtypeStruct((b, t, d), BF16),
        jax.ShapeDtypeStruct((b, t, d), BF16),
    ]
    out_specs = [
        row(d),
        pl.BlockSpec((1, CONV_WIDTH - 1, d), lambda i, j: (i, 0, 0)),
        row(d), row(d), row(d), row(d),
    ]
    if transposed_kv:
        out_shape += [jax.ShapeDtypeStruct((b, d, t), BF16)] * 2
        out_specs += [pl.BlockSpec((1, d, tm), lambda i, j: (i, 0, j))] * 2
    else:
        out_shape.append(jax.ShapeDtypeStruct((b, t, d), BF16))
        out_specs.append(row(d))
    return pl.pallas_call(
        functools.partial(_trunk_a_kernel, tm=tm, transposed_kv=transposed_kv),
        grid=(b, nt),
        in_specs=in_specs,
        out_specs=out_specs,
        out_shape=out_shape,
        scratch_shapes=[pltpu.VMEM((CONV_PAD + tm, d), F32)],
        compiler_params=pltpu.CompilerParams(
            dimension_semantics=("arbitrary", "arbitrary"), vmem_limit_bytes=VMEM_LIMIT),
        name="trunk_a_kt" if transposed_kv else "trunk_a",
    )(x, conv_state, gpre, win, cw, wout, gpost, gkv, wk, wv, gb, bwin)


def _neg_abs(x):
    bits = lax.bitcast_convert_type(x, jnp.uint16) | jnp.uint16(0x8000)
    return lax.bitcast_convert_type(bits, BF16)


def _sb_weights(s_chunk, n_chunks, c, tri, causal, emit=None):
    chunk = tri.shape[0]
    last_lane = lax.broadcasted_iota(jnp.int32, c.shape, 1) == LANES - 1
    ps = []
    for j in reversed(range(n_chunks)):
        sb = s_chunk(j).astype(BF16)
        e = jnp.exp2(_neg_abs(sb))
        t = 1.0 + e
        r = e - (t - 1.0)
        sp_j = jnp.maximum(sb, jnp.zeros_like(sb)) + (jnp.log(t) + r) * LOG2E
        if causal is not None:
            sp_j = jnp.where(causal, sp_j, jnp.zeros_like(sp_j))
        own = sb - sp_j
        later = _dot(sp_j, tri)
        tail = later[:, chunk - LANES:]
        parts = [later[:, t * LANES:(t + 1) * LANES] + c for t in range(chunk // LANES - 1)]
        parts.append(jnp.where(last_lane, c, tail + c))
        behind = jnp.concatenate(parts, axis=1).astype(BF16)
        p = jnp.exp2(own - behind)
        if causal is not None:
            p = jnp.where(causal, p, jnp.zeros_like(p))
        if emit is None:
            ps.append(p)
        else:
            emit(j, p)
        c = c + jnp.broadcast_to(tail[:, LANES - 1:], c.shape)
    return ps[::-1], c


def _split_heads(q2):
    lane = lax.broadcasted_iota(jnp.int32, q2.shape, 1)
    zero = jnp.zeros_like(q2)
    return jnp.where(lane < HEAD_DIM, q2, zero), jnp.where(lane >= HEAD_DIM, q2, zero)


def _merge_heads(o0, o1):
    lane = lax.broadcasted_iota(jnp.int32, o0.shape, 1)
    return jnp.where(lane < HEAD_DIM, o0, o1)


def _attn_prompt_kernel(q_ref, kt_ref, vt_ref, tri_ref, o_ref, q_scr, acc_ref, c_ref, *, tq, tk, kb):
    i = pl.program_id(2)
    tri = tri_ref[...]
    n_chunks = tq // tk
    for r in range(n_chunks):
        heads = _split_heads(q_ref[0, r * tk:(r + 1) * tk, :])
        for h in range(HEADS_PER_STEP):
            q_scr[(HEADS_PER_STEP * r + h) * tk:(HEADS_PER_STEP * r + h + 1) * tk, :] = heads[h]
    acc_ref[...] = jnp.zeros_like(acc_ref)
    c_ref[...] = jnp.zeros_like(c_ref)

    def block(key_start, n_keys, row0, causal):
        kt = kt_ref[0, :, pl.ds(key_start, n_keys)]
        q_rows = q_scr[row0:, :]
        groups = range(row0 // tk, HEADS_PER_STEP * n_chunks)
        partial = {r: [] for r in groups}

        def emit(j, p):
            for r in groups:
                h = r % HEADS_PER_STEP
                vt = vt_ref[0, h * HEAD_DIM:(h + 1) * HEAD_DIM, pl.ds(key_start + j * tk, tk)]
                g = r - row0 // tk
                partial[r].append(_dot_nt(vt, p[g * tk:(g + 1) * tk, :]))

        _, c_new = _sb_weights(lambda j: _dot(q_rows, kt[:, j * tk:(j + 1) * tk]), n_keys // tk,
                               c_ref[row0:, :], tri, causal, emit)
        c_ref[row0:, :] = c_new
        for r in groups:
            acc_ref[r] += sum(partial[r][1:], partial[r][0])

    q0 = pl.multiple_of(i * tq, tq)
    for d in reversed(range(n_chunks)):
        row0 = HEADS_PER_STEP * d * tk
        rows = HEADS_PER_STEP * tq - row0
        row = lax.broadcasted_iota(jnp.int32, (rows, tk), 0)
        col = lax.broadcasted_iota(jnp.int32, (rows, tk), 1)
        causal = col < jnp.where(row < HEADS_PER_STEP * tk, row & (tk - 1), tk)
        block(q0 + d * tk, tk, row0, causal)

    def body(j, carry):
        block(pl.multiple_of(q0 - (j + 1) * kb, kb), kb, 0, None)
        return carry

    lax.fori_loop(0, i * (tq // kb), body, 0)
    for r in range(n_chunks):
        o_t = jnp.concatenate([acc_ref[HEADS_PER_STEP * r + h] for h in range(HEADS_PER_STEP)], axis=0)
        o_ref[0, r * tk:(r + 1) * tk, :] = o_t.T.astype(o_ref.dtype)


def _tri_matrix(n):
    j = lax.broadcasted_iota(jnp.int32, (n, n), 0)
    k = lax.broadcasted_iota(jnp.int32, (n, n), 1)
    return ((j > k) | (k == n - 1)).astype(BF16)


def _attn_prompt(q, kt, vt):
    b, t, d = q.shape
    tq = min(Q_TILE, t)
    tk = min(K_TILE, tq)
    kb = min(K_BLOCK, tq)
    assert t % tq == 0 and tq % tk == 0 and tq % kb == 0 and kb % tk == 0
    return pl.pallas_call(
        functools.partial(_attn_prompt_kernel, tq=tq, tk=tk, kb=kb),
        grid=(b, d // PAIR, t // tq),
        in_specs=[
            pl.BlockSpec((1, tq, PAIR), lambda bi, hp, qi: (bi, qi, hp)),
            pl.BlockSpec((1, PAIR, t), lambda bi, hp, qi: (bi, hp, 0)),
            pl.BlockSpec((1, PAIR, t), lambda bi, hp, qi: (bi, hp, 0)),
            _const_spec((tk, tk)),
        ],
        out_specs=pl.BlockSpec((1, tq, PAIR), lambda bi, hp, qi: (bi, qi, hp)),
        out_shape=jax.ShapeDtypeStruct((b, t, d), BF16),
        scratch_shapes=[pltpu.VMEM((HEADS_PER_STEP * tq, PAIR), BF16),
                        pltpu.VMEM((HEADS_PER_STEP * tq // tk, HEAD_DIM, tk), F32),
                        pltpu.VMEM((HEADS_PER_STEP * tq, LANES), F32)],
        compiler_params=pltpu.CompilerParams(
            dimension_semantics=("arbitrary", "arbitrary", "arbitrary"), vmem_limit_bytes=VMEM_LIMIT),
        name="attn_prompt",
    )(q, kt, vt, _tri_matrix(tk))


def _attn_sample_kernel(q_ref, kn_ref, vn_ref, pk_ref, pv_ref, tri_new_ref, tri_ref, o_ref):
    tn = q_ref.shape[1]
    q_rows = jnp.concatenate(_split_heads(q_ref[0]), axis=0)
    pad = jnp.zeros((NEW_KEY_PAD - tn, PAIR), BF16)
    kn = jnp.concatenate([kn_ref[0].astype(BF16), pad], axis=0)
    vn = jnp.concatenate([vn_ref[0], pad], axis=0)
    row = lax.broadcasted_iota(jnp.int32, (HEADS_PER_STEP * tn, NEW_KEY_PAD), 0)
    col = lax.broadcasted_iota(jnp.int32, (HEADS_PER_STEP * tn, NEW_KEY_PAD), 1)
    causal = col < jnp.where(row < tn, row, row - tn)
    c = jnp.zeros((HEADS_PER_STEP * tn, LANES), F32)
    (p_new,), c = _sb_weights(lambda j: _dot_nt(q_rows, kn), 1, c, tri_new_ref[...], causal)
    kp = pk_ref[0].astype(BF16)
    p_past, _ = _sb_weights(lambda j: _dot_nt(q_rows, kp[j * K_TILE:(j + 1) * K_TILE]), kp.shape[0] // K_TILE,
                            c, tri_ref[...], None)
    o = _dot(p_new, vn) + _dot(jnp.concatenate(p_past, axis=1), pv_ref[0].astype(BF16))
    o_ref[0] = _merge_heads(o[:tn], o[tn:]).astype(o_ref.dtype)


def _attn_sample(q, k_new, vb_new, past_k, past_v):
    b, tn, d = q.shape
    past = past_k.shape[1]
    assert HEADS_PER_STEP == 2 and past % K_TILE == 0 and tn <= NEW_KEY_PAD and tn % 16 == 0
    blk = lambda rows: pl.BlockSpec((1, rows, PAIR), lambda bi, hp: (bi, 0, hp))
    return pl.pallas_call(
        _attn_sample_kernel,
        grid=(b, d // PAIR),
        in_specs=[blk(tn), blk(tn), blk(tn), blk(past), blk(past),
                  _const_spec((NEW_KEY_PAD, NEW_KEY_PAD)), _const_spec((K_TILE, K_TILE))],
        out_specs=blk(tn),
        out_shape=jax.ShapeDtypeStruct((b, tn, d), BF16),
        compiler_params=pltpu.CompilerParams(
            dimension_semantics=("arbitrary", "arbitrary"), vmem_limit_bytes=VMEM_LIMIT),
        name="attn_sample",
    )(q, k_new, vb_new, past_k.reshape(b, past, d), past_v.reshape(b, past, d),
      _tri_matrix(NEW_KEY_PAD), _tri_matrix(K_TILE))


def _out_kernel(o_ref, sz_ref, x1_ref, w_ref, g_ref, y_ref):
    gated = (o_ref[0].astype(F32) * sz_ref[0].astype(F32)).astype(BF16)
    h = _dot(gated, w_ref[...])
    y_ref[0] = x1_ref[0] + h * _inv_rms(h) * g_ref[...]


def _out_proj(o, sz, x1, w, g):
    b, t, d = x1.shape
    tm = ROW_TILE if t % ROW_TILE == 0 else t
    row = pl.BlockSpec((1, tm, d), lambda i, j: (i, j, 0))
    return pl.pallas_call(
        _out_kernel,
        grid=(b, t // tm),
        in_specs=[row, row, row, _const_spec((d, d)), _const_spec((1, d))],
        out_specs=row,
        out_shape=jax.ShapeDtypeStruct((b, t, d), F32),
        compiler_params=pltpu.CompilerParams(
            dimension_semantics=("arbitrary", "arbitrary"), vmem_limit_bytes=VMEM_LIMIT),
        name="out_proj",
    )(o, sz, x1, w, g)


def _trunk(x, conv_state, past_k, past_v, w):
    b, t, d = x.shape
    outs = _trunk_a(x, conv_state, *w["a"], transposed_kv=past_k is None)
    x1, cso, k, v, q, sz = outs[:6]
    if past_k is None:
        o = _attn_prompt(q, outs[6], outs[7])
    else:
        o = _attn_sample(q, k, outs[6], past_k, past_v)
    y = _out_proj(o, sz, x1, *w["out"])
    return (y, cso[None], k.reshape(b, t, N_HEADS, HEAD_DIM), v.reshape(b, t, N_HEADS, HEAD_DIM))


def kernel(x_prompt, x_sample, state_conv, cache_k, cache_v, a_norm_pre, a_w_in, a_conv_w, a_w_out, a_norm_post,
           kv_norm, w_k, w_v, b_norm_pre, b_w_in, b_w_out, b_norm_post):
    assert a_w_in.shape[0] == 1 and b_w_in.shape[0] == 1, "one conv layer followed by one attention layer"
    d = x_prompt.shape[-1]
    assert d == N_HEADS * HEAD_DIM
    vec = lambda g: g.reshape(1, d).astype(F32)
    w = {
        "a": (vec(a_norm_pre[0]), a_w_in[0].astype(BF16), a_conv_w[0].astype(F32), a_w_out[0].astype(BF16),
              vec(a_norm_post[0]), vec(kv_norm), w_k.astype(BF16), w_v.astype(BF16),
              vec(b_norm_pre[0]), b_w_in[0].astype(BF16)),
        "out": (b_w_out[0].astype(BF16), vec(b_norm_post[0])),
    }
    conv0 = jnp.zeros((x_prompt.shape[0], CONV_WIDTH - 1, d), x_prompt.dtype)
    y_p, cs_p, k_p, v_p = _trunk(x_prompt, conv0, None, None, w)
    y_s, cs_s, k_s, v_s = _trunk(x_sample, state_conv[0], cache_k, cache_v, w)
    return (y_p, y_s, cs_p, cs_s, k_p, v_p, k_s, v_s)
```

```python
import functools
import math

import jax
import jax.numpy as jnp
from jax import lax
from jax.experimental import pallas as pl
from jax.experimental.pallas import tpu as pltpu

N_HEADS = 16
HEAD_DIM = 64
HEADS_PER_STEP = 2
PAIR = HEADS_PER_STEP * HEAD_DIM
LANES = 128
EPS = 1e-6
CONV_WIDTH = 3
CONV_PAD = 8
LOG2E = math.log2(math.e)
Q_SCALE = (HEAD_DIM ** -0.5) * LOG2E
ROW_TILE = 512
Q_TILE = 2048
K_TILE = 256
K_BLOCK = 1024
NEW_KEY_PAD = 128
VMEM_LIMIT = 56 * 1024 * 1024

F32 = jnp.float32
BF16 = jnp.bfloat16


def _dot(a, b):
    return jnp.dot(a, b, preferred_element_type=F32)


def _dot_nt(a, b):
    return lax.dot_general(a, b, (((1,), (1,)), ((), ())), preferred_element_type=F32)


def _inv_rms(x):
    return lax.rsqrt(jnp.mean(x * x, axis=-1, keepdims=True) + EPS)


def _const_spec(shape):
    nd = len(shape)
    return pl.BlockSpec(shape, lambda *_: (0,) * nd, pipeline_mode=pl.Buffered(1))


def _trunk_a_kernel(x_ref, cs_ref, gpre_ref, win_ref, cw_ref, wout_ref, gpost_ref,
                    gkv_ref, wk_ref, wv_ref, gb_ref, bwin_ref,
                    x1_ref, cso_ref, k_ref, v_ref, q_ref, sz_ref, vb_ref, *rest, tm, emit_kt):
    if emit_kt:
        kt_ref, u_scr = rest
    else:
        (u_scr,) = rest
    d = x_ref.shape[-1]
    t = pl.program_id(1)

    @pl.when(t == 0)
    def _():
        u_scr[CONV_PAD - (CONV_WIDTH - 1):CONV_PAD, :] = cs_ref[0]

    x = x_ref[0]
    xn = (x * _inv_rms(x) * gpre_ref[...]).astype(BF16)
    c_gate = _dot(xn, win_ref[:, d:2 * d])
    h_in = _dot(xn, win_ref[:, 2 * d:3 * d])
    u = c_gate * h_in
    u_scr[CONV_PAD:CONV_PAD + tm, :] = u
    conv = cw_ref[CONV_WIDTH - 1:CONV_WIDTH, :] * u
    for i in range(CONV_WIDTH - 1):
        off = CONV_PAD - (CONV_WIDTH - 1) + i
        conv = conv + cw_ref[i:i + 1, :] * u_scr[off:off + tm, :]
    tail = u_scr[tm + CONV_PAD - (CONV_WIDTH - 1):tm + CONV_PAD, :]
    cso_ref[0] = tail
    u_scr[CONV_PAD - (CONV_WIDTH - 1):CONV_PAD, :] = tail

    b_gate = _dot(xn, win_ref[:, 0:d])
    z = _dot(xn, win_ref[:, 3 * d:4 * d])
    y = (z / (1.0 + jnp.exp(-z))) * b_gate * conv
    h = _dot(y.astype(BF16), wout_ref[...])
    x1 = x + h * _inv_rms(h) * gpost_ref[...]
    x1_ref[0] = x1

    xs = x1 * _inv_rms(x1)
    kv_in = (xs * gkv_ref[...]).astype(BF16)
    qn = (xs * gb_ref[...]).astype(BF16)
    k = _dot(kv_in, wk_ref[...])
    v = _dot(kv_in, wv_ref[...])
    k_ref[0] = k
    v_ref[0] = v
    vb_ref[0] = v.astype(BF16)
    if emit_kt:
        kt_ref[0] = k.T.astype(BF16)
    q = _dot(qn, bwin_ref[:, 0:d])
    q_ref[0] = (q * Q_SCALE).astype(BF16)
    zg = _dot(qn, bwin_ref[:, d:2 * d])
    sz_ref[0] = (zg / (1.0 + jnp.exp(-zg))).astype(BF16)


def _trunk_a(x, conv_state, gpre, win, cw, wout, gpost, gkv, wk, wv, gb, bwin, *, emit_kt):
    b, t, d = x.shape
    tm = ROW_TILE if t % ROW_TILE == 0 else t
    nt = t // tm
    row = lambda shape_last: pl.BlockSpec((1, tm, shape_last), lambda i, j: (i, j, 0))
    in_specs = [
        row(d),
        pl.BlockSpec((1, CONV_WIDTH - 1, d), lambda i, j: (i, 0, 0)),
        _const_spec((1, d)), _const_spec((d, 4 * d)), _const_spec((CONV_WIDTH, d)), _const_spec((d, d)),
        _const_spec((1, d)), _const_spec((1, d)), _const_spec((d, d)), _const_spec((d, d)),
        _const_spec((1, d)), _const_spec((d, 2 * d)),
    ]
    out_shape = [
        jax.ShapeDtypeStruct((b, t, d), F32),
        jax.ShapeDtypeStruct((b, CONV_WIDTH - 1, d), F32),
        jax.ShapeDtypeStruct((b, t, d), F32),
        jax.ShapeDtypeStruct((b, t, d), F32),
        jax.ShapeDtypeStruct((b, t, d), BF16),
        jax.ShapeDtypeStruct((b, t, d), BF16),
        jax.ShapeDtypeStruct((b, t, d), BF16),
    ]
    out_specs = [
        row(d),
        pl.BlockSpec((1, CONV_WIDTH - 1, d), lambda i, j: (i, 0, 0)),
        row(d), row(d), row(d), row(d), row(d),
    ]
    if emit_kt:
        out_shape.append(jax.ShapeDtypeStruct((b, d, t), BF16))
        out_specs.append(pl.BlockSpec((1, d, tm), lambda i, j: (i, 0, j)))
    return pl.pallas_call(
        functools.partial(_trunk_a_kernel, tm=tm, emit_kt=emit_kt),
        grid=(b, nt),
        in_specs=in_specs,
        out_specs=out_specs,
        out_shape=out_shape,
        scratch_shapes=[pltpu.VMEM((CONV_PAD + tm, d), F32)],
        compiler_params=pltpu.CompilerParams(
            dimension_semantics=("arbitrary", "arbitrary"), vmem_limit_bytes=VMEM_LIMIT),
        name="trunk_a_kt" if emit_kt else "trunk_a",
    )(x, conv_state, gpre, win, cw, wout, gpost, gkv, wk, wv, gb, bwin)


def _neg_abs(x):
    bits = lax.bitcast_convert_type(x, jnp.uint16) | jnp.uint16(0x8000)
    return lax.bitcast_convert_type(bits, BF16)


def _sb_weights(s_chunk, n_chunks, c, tri, causal):
    chunk = tri.shape[0]
    last_lane = lax.broadcasted_iota(jnp.int32, c.shape, 1) == LANES - 1
    ps = []
    for j in reversed(range(n_chunks)):
        sb = s_chunk(j).astype(BF16)
        e = jnp.exp2(_neg_abs(sb))
        t = 1.0 + e
        r = e - (t - 1.0)
        sp_j = jnp.maximum(sb, jnp.zeros_like(sb)) + (jnp.log(t) + r) * LOG2E
        if causal is not None:
            sp_j = jnp.where(causal, sp_j, jnp.zeros_like(sp_j))
        own = sb - sp_j
        later = _dot(sp_j, tri)
        tail = later[:, chunk - LANES:]
        parts = [later[:, t * LANES:(t + 1) * LANES] + c for t in range(chunk // LANES - 1)]
        parts.append(jnp.where(last_lane, c, tail + c))
        behind = jnp.concatenate(parts, axis=1).astype(BF16)
        p = jnp.exp2(own - behind)
        if causal is not None:
            p = jnp.where(causal, p, jnp.zeros_like(p))
        ps.append(p)
        c = c + jnp.broadcast_to(tail[:, LANES - 1:], c.shape)
    return ps[::-1], c


def _split_heads(q2):
    lane = lax.broadcasted_iota(jnp.int32, q2.shape, 1)
    zero = jnp.zeros_like(q2)
    return jnp.where(lane < HEAD_DIM, q2, zero), jnp.where(lane >= HEAD_DIM, q2, zero)


def _merge_heads(o0, o1):
    lane = lax.broadcasted_iota(jnp.int32, o0.shape, 1)
    return jnp.where(lane < HEAD_DIM, o0, o1)


def _attn_prompt_kernel(q_ref, kt_ref, v_ref, tri_ref, o_ref, q_scr, acc_ref, c_ref, *, tq, tk, kb):
    i = pl.program_id(2)
    tri = tri_ref[...]
    n_chunks = tq // tk
    for r in range(n_chunks):
        heads = _split_heads(q_ref[0, r * tk:(r + 1) * tk, :])
        for h in range(HEADS_PER_STEP):
            q_scr[(HEADS_PER_STEP * r + h) * tk:(HEADS_PER_STEP * r + h + 1) * tk, :] = heads[h]
    acc_ref[...] = jnp.zeros_like(acc_ref)
    c_ref[...] = jnp.zeros_like(c_ref)

    def weights(key_start, n_keys, row0, causal):
        kt = kt_ref[0, :, pl.ds(key_start, n_keys)]
        q_rows = q_scr[row0:, :]
        ps, c_new = _sb_weights(lambda j: _dot(q_rows, kt[:, j * tk:(j + 1) * tk]), n_keys // tk,
                                c_ref[row0:, :], tri, causal)
        c_ref[row0:, :] = c_new
        return ps

    def apply(p, key_start, n_keys, row0):
        acc_ref[row0:, :] += _dot(p, v_ref[0, pl.ds(key_start, n_keys), :])

    q0 = pl.multiple_of(i * tq, tq)
    for d in reversed(range(n_chunks)):
        row0 = HEADS_PER_STEP * d * tk
        rows = HEADS_PER_STEP * tq - row0
        row = lax.broadcasted_iota(jnp.int32, (rows, tk), 0)
        col = lax.broadcasted_iota(jnp.int32, (rows, tk), 1)
        causal = col < jnp.where(row < HEADS_PER_STEP * tk, row & (tk - 1), tk)
        (p,) = weights(q0 + d * tk, tk, row0, causal)
        apply(p, q0 + d * tk, tk, row0)

    def body(j, carry):
        start = pl.multiple_of(q0 - (j + 1) * kb, kb)
        apply(jnp.concatenate(weights(start, kb, 0, None), axis=1), start, kb, 0)
        return carry

    lax.fori_loop(0, i * (tq // kb), body, 0)
    for r in range(n_chunks):
        base = HEADS_PER_STEP * r * tk
        o_ref[0, r * tk:(r + 1) * tk, :] = _merge_heads(
            acc_ref[base:base + tk, :], acc_ref[base + tk:base + 2 * tk, :]).astype(o_ref.dtype)


def _tri_matrix(n):
    j = lax.broadcasted_iota(jnp.int32, (n, n), 0)
    k = lax.broadcasted_iota(jnp.int32, (n, n), 1)
    return ((j > k) | (k == n - 1)).astype(BF16)


def _attn_prompt(q, kt, vb):
    b, t, d = q.shape
    tq = min(Q_TILE, t)
    tk = min(K_TILE, tq)
    kb = min(K_BLOCK, tq)
    assert t % tq == 0 and tq % tk == 0 and tq % kb == 0 and kb % tk == 0
    return pl.pallas_call(
        functools.partial(_attn_prompt_kernel, tq=tq, tk=tk, kb=kb),
        grid=(b, d // PAIR, t // tq),
        in_specs=[
            pl.BlockSpec((1, tq, PAIR), lambda bi, hp, qi: (bi, qi, hp)),
            pl.BlockSpec((1, PAIR, t), lambda bi, hp, qi: (bi, hp, 0)),
            pl.BlockSpec((1, t, PAIR), lambda bi, hp, qi: (bi, 0, hp)),
            _const_spec((tk, tk)),
        ],
        out_specs=pl.BlockSpec((1, tq, PAIR), lambda bi, hp, qi: (bi, qi, hp)),
        out_shape=jax.ShapeDtypeStruct((b, t, d), BF16),
        scratch_shapes=[pltpu.VMEM((HEADS_PER_STEP * tq, PAIR), BF16),
                        pltpu.VMEM((HEADS_PER_STEP * tq, PAIR), F32),
                        pltpu.VMEM((HEADS_PER_STEP * tq, LANES), F32)],
        compiler_params=pltpu.CompilerParams(
            dimension_semantics=("arbitrary", "arbitrary", "arbitrary"), vmem_limit_bytes=VMEM_LIMIT),
        name="attn_prompt",
    )(q, kt, vb, _tri_matrix(tk))


def _attn_sample_kernel(q_ref, kn_ref, vn_ref, pk_ref, pv_ref, tri_new_ref, tri_ref, o_ref):
    tn = q_ref.shape[1]
    q_rows = jnp.concatenate(_split_heads(q_ref[0]), axis=0)
    pad = jnp.zeros((NEW_KEY_PAD - tn, PAIR), BF16)
    kn = jnp.concatenate([kn_ref[0].astype(BF16), pad], axis=0)
    vn = jnp.concatenate([vn_ref[0], pad], axis=0)
    row = lax.broadcasted_iota(jnp.int32, (HEADS_PER_STEP * tn, NEW_KEY_PAD), 0)
    col = lax.broadcasted_iota(jnp.int32, (HEADS_PER_STEP * tn, NEW_KEY_PAD), 1)
    causal = col < jnp.where(row < tn, row, row - tn)
    c = jnp.zeros((HEADS_PER_STEP * tn, LANES), F32)
    (p_new,), c = _sb_weights(lambda j: _dot_nt(q_rows, kn), 1, c, tri_new_ref[...], causal)
    kp = pk_ref[0].astype(BF16)
    p_past, _ = _sb_weights(lambda j: _dot_nt(q_rows, kp[j * K_TILE:(j + 1) * K_TILE]), kp.shape[0] // K_TILE,
                            c, tri_ref[...], None)
    o = _dot(p_new, vn) + _dot(jnp.concatenate(p_past, axis=1), pv_ref[0].astype(BF16))
    o_ref[0] = _merge_heads(o[:tn], o[tn:]).astype(o_ref.dtype)


def _attn_sample(q, k_new, vb_new, past_k, past_v):
    b, tn, d = q.shape
    past = past_k.shape[1]
    assert HEADS_PER_STEP == 2 and past % K_TILE == 0 and tn <= NEW_KEY_PAD and tn % 16 == 0
    blk = lambda rows: pl.BlockSpec((1, rows, PAIR), lambda bi, hp: (bi, 0, hp))
    return pl.pallas_call(
        _attn_sample_kernel,
        grid=(b, d // PAIR),
        in_specs=[blk(tn), blk(tn), blk(tn), blk(past), blk(past),
                  _const_spec((NEW_KEY_PAD, NEW_KEY_PAD)), _const_spec((K_TILE, K_TILE))],
        out_specs=blk(tn),
        out_shape=jax.ShapeDtypeStruct((b, tn, d), BF16),
        compiler_params=pltpu.CompilerParams(
            dimension_semantics=("arbitrary", "arbitrary"), vmem_limit_bytes=VMEM_LIMIT),
        name="attn_sample",
    )(q, k_new, vb_new, past_k.reshape(b, past, d), past_v.reshape(b, past, d),
      _tri_matrix(NEW_KEY_PAD), _tri_matrix(K_TILE))


def _out_kernel(o_ref, sz_ref, x1_ref, w_ref, g_ref, y_ref):
    gated = (o_ref[0].astype(F32) * sz_ref[0].astype(F32)).astype(BF16)
    h = _dot(gated, w_ref[...])
    y_ref[0] = x1_ref[0] + h * _inv_rms(h) * g_ref[...]


def _out_proj(o, sz, x1, w, g):
    b, t, d = x1.shape
    tm = ROW_TILE if t % ROW_TILE == 0 else t
    row = pl.BlockSpec((1, tm, d), lambda i, j: (i, j, 0))
    return pl.pallas_call(
        _out_kernel,
        grid=(b, t // tm),
        in_specs=[row, row, row, _const_spec((d, d)), _const_spec((1, d))],
        out_specs=row,
        out_shape=jax.ShapeDtypeStruct((b, t, d), F32),
        compiler_params=pltpu.CompilerParams(
            dimension_semantics=("arbitrary", "arbitrary"), vmem_limit_bytes=VMEM_LIMIT),
        name="out_proj",
    )(o, sz, x1, w, g)


def _trunk(x, conv_state, past_k, past_v, w):
    b, t, d = x.shape
    outs = _trunk_a(x, conv_state, *w["a"], emit_kt=past_k is None)
    x1, cso, k, v, q, sz, vb = outs[:7]
    if past_k is None:
        o = _attn_prompt(q, outs[7], vb)
    else:
        o = _attn_sample(q, k, vb, past_k, past_v)
    y = _out_proj(o, sz, x1, *w["out"])
    return (y, cso[None], k.reshape(b, t, N_HEADS, HEAD_DIM), v.reshape(b, t, N_HEADS, HEAD_DIM))


def kernel(x_prompt, x_sample, state_conv, cache_k, cache_v, a_norm_pre, a_w_in, a_conv_w, a_w_out, a_norm_post,
           kv_norm, w_k, w_v, b_norm_pre, b_w_in, b_w_out, b_norm_post):
    assert a_w_in.shape[0] == 1 and b_w_in.shape[0] == 1, "one conv layer followed by one attention layer"
    d = x_prompt.shape[-1]
    assert d == N_HEADS * HEAD_DIM
    vec = lambda g: g.reshape(1, d).astype(F32)
    w = {
        "a": (vec(a_norm_pre[0]), a_w_in[0].astype(BF16), a_conv_w[0].astype(F32), a_w_out[0].astype(BF16),
              vec(a_norm_post[0]), vec(kv_norm), w_k.astype(BF16), w_v.astype(BF16),
              vec(b_norm_pre[0]), b_w_in[0].astype(BF16)),
        "out": (b_w_out[0].astype(BF16), vec(b_norm_post[0])),
    }
    conv0 = jnp.zeros((x_prompt.shape[0], CONV_WIDTH - 1, d), x_prompt.dtype)
    y_p, cs_p, k_p, v_p = _trunk(x_prompt, conv0, None, None, w)
    y_s, cs_s, k_s, v_s = _trunk(x_sample, state_conv[0], cache_k, cache_v, w)
    return (y_p, y_s, cs_p, cs_s, k_p, v_p, k_s, v_s)
```

```python
import functools
import math

import jax
import jax.numpy as jnp
from jax import lax
from jax.experimental import pallas as pl
from jax.experimental.pallas import tpu as pltpu

N_HEADS = 16
HEAD_DIM = 64
HEADS_PER_STEP = 2
PAIR = HEADS_PER_STEP * HEAD_DIM
LANES = 128
EPS = 1e-6
CONV_WIDTH = 3
CONV_PAD = 8
LOG2E = math.log2(math.e)
Q_SCALE = (HEAD_DIM ** -0.5) * LOG2E
ROW_TILE = 512
Q_TILE = 2048
K_TILE = 256
K_BLOCK = 2048
NEW_KEY_PAD = 128
VMEM_LIMIT = 56 * 1024 * 1024

F32 = jnp.float32
BF16 = jnp.bfloat16


def _dot(a, b):
    return jnp.dot(a, b, preferred_element_type=F32)


def _dot_nt(a, b):
    return lax.dot_general(a, b, (((1,), (1,)), ((), ())), preferred_element_type=F32)


def _inv_rms(x):
    return lax.rsqrt(jnp.mean(x * x, axis=-1, keepdims=True) + EPS)


def _const_spec(shape):
    nd = len(shape)
    return pl.BlockSpec(shape, lambda *_: (0,) * nd, pipeline_mode=pl.Buffered(1))


def _trunk_a_kernel(x_ref, cs_ref, gpre_ref, win_ref, cw_ref, wout_ref, gpost_ref,
                    gkv_ref, wk_ref, wv_ref, gb_ref, bwin_ref,
                    x1_ref, cso_ref, k_ref, v_ref, q_ref, sz_ref, vb_ref, *rest, tm, emit_kt):
    if emit_kt:
        kt_ref, u_scr = rest
    else:
        (u_scr,) = rest
    d = x_ref.shape[-1]
    t = pl.program_id(1)

    @pl.when(t == 0)
    def _():
        u_scr[CONV_PAD - (CONV_WIDTH - 1):CONV_PAD, :] = cs_ref[0]

    x = x_ref[0]
    xn = (x * _inv_rms(x) * gpre_ref[...]).astype(BF16)
    c_gate = _dot(xn, win_ref[:, d:2 * d])
    h_in = _dot(xn, win_ref[:, 2 * d:3 * d])
    u = c_gate * h_in
    u_scr[CONV_PAD:CONV_PAD + tm, :] = u
    conv = cw_ref[CONV_WIDTH - 1:CONV_WIDTH, :] * u
    for i in range(CONV_WIDTH - 1):
        off = CONV_PAD - (CONV_WIDTH - 1) + i
        conv = conv + cw_ref[i:i + 1, :] * u_scr[off:off + tm, :]
    tail = u_scr[tm + CONV_PAD - (CONV_WIDTH - 1):tm + CONV_PAD, :]
    cso_ref[0] = tail
    u_scr[CONV_PAD - (CONV_WIDTH - 1):CONV_PAD, :] = tail

    b_gate = _dot(xn, win_ref[:, 0:d])
    z = _dot(xn, win_ref[:, 3 * d:4 * d])
    y = (z / (1.0 + jnp.exp(-z))) * b_gate * conv
    h = _dot(y.astype(BF16), wout_ref[...])
    x1 = x + h * _inv_rms(h) * gpost_ref[...]
    x1_ref[0] = x1

    xs = x1 * _inv_rms(x1)
    kv_in = (xs * gkv_ref[...]).astype(BF16)
    qn = (xs * gb_ref[...]).astype(BF16)
    k = _dot(kv_in, wk_ref[...])
    v = _dot(kv_in, wv_ref[...])
    k_ref[0] = k
    v_ref[0] = v
    vb_ref[0] = v.astype(BF16)
    if emit_kt:
        kt_ref[0] = k.T.astype(BF16)
    q = _dot(qn, bwin_ref[:, 0:d])
    q_ref[0] = (q * Q_SCALE).astype(BF16)
    zg = _dot(qn, bwin_ref[:, d:2 * d])
    sz_ref[0] = (zg / (1.0 + jnp.exp(-zg))).astype(BF16)


def _trunk_a(x, conv_state, gpre, win, cw, wout, gpost, gkv, wk, wv, gb, bwin, *, emit_kt):
    b, t, d = x.shape
    tm = ROW_TILE if t % ROW_TILE == 0 else t
    nt = t // tm
    row = lambda shape_last: pl.BlockSpec((1, tm, shape_last), lambda i, j: (i, j, 0))
    in_specs = [
        row(d),
        pl.BlockSpec((1, CONV_WIDTH - 1, d), lambda i, j: (i, 0, 0)),
        _const_spec((1, d)), _const_spec((d, 4 * d)), _const_spec((CONV_WIDTH, d)), _const_spec((d, d)),
        _const_spec((1, d)), _const_spec((1, d)), _const_spec((d, d)), _const_spec((d, d)),
        _const_spec((1, d)), _const_spec((d, 2 * d)),
    ]
    out_shape = [
        jax.ShapeDtypeStruct((b, t, d), F32),
        jax.ShapeDtypeStruct((b, CONV_WIDTH - 1, d), F32),
        jax.ShapeDtypeStruct((b, t, d), F32),
        jax.ShapeDtypeStruct((b, t, d), F32),
        jax.ShapeDtypeStruct((b, t, d), BF16),
        jax.ShapeDtypeStruct((b, t, d), BF16),
        jax.ShapeDtypeStruct((b, t, d), BF16),
    ]
    out_specs = [
        row(d),
        pl.BlockSpec((1, CONV_WIDTH - 1, d), lambda i, j: (i, 0, 0)),
        row(d), row(d), row(d), row(d), row(d),
    ]
    if emit_kt:
        out_shape.append(jax.ShapeDtypeStruct((b, d, t), BF16))
        out_specs.append(pl.BlockSpec((1, d, tm), lambda i, j: (i, 0, j)))
    return pl.pallas_call(
        functools.partial(_trunk_a_kernel, tm=tm, emit_kt=emit_kt),
        grid=(b, nt),
        in_specs=in_specs,
        out_specs=out_specs,
        out_shape=out_shape,
        scratch_shapes=[pltpu.VMEM((CONV_PAD + tm, d), F32)],
        compiler_params=pltpu.CompilerParams(
            dimension_semantics=("arbitrary", "arbitrary"), vmem_limit_bytes=VMEM_LIMIT),
        name="trunk_a_kt" if emit_kt else "trunk_a",
    )(x, conv_state, gpre, win, cw, wout, gpost, gkv, wk, wv, gb, bwin)


def _neg_abs(x):
    bits = lax.bitcast_convert_type(x, jnp.uint16) | jnp.uint16(0x8000)
    return lax.bitcast_convert_type(bits, BF16)


def _sb_weights(s_chunk, n_chunks, c, tri, causal):
    chunk = tri.shape[0]
    last_lane = lax.broadcasted_iota(jnp.int32, c.shape, 1) == LANES - 1
    ps = []
    for j in reversed(range(n_chunks)):
        sb = s_chunk(j).astype(BF16)
        e = jnp.exp2(_neg_abs(sb))
        t = 1.0 + e
        r = e - (t - 1.0)
        sp_j = jnp.maximum(sb, jnp.zeros_like(sb)) + (jnp.log(t) + r) * LOG2E
        if causal is not None:
            sp_j = jnp.where(causal, sp_j, jnp.zeros_like(sp_j))
        own = sb - sp_j
        later = _dot(sp_j, tri)
        tail = later[:, chunk - LANES:]
        parts = [later[:, t * LANES:(t + 1) * LANES] + c for t in range(chunk // LANES - 1)]
        parts.append(jnp.where(last_lane, c, tail + c))
        behind = jnp.concatenate(parts, axis=1).astype(BF16)
        p = jnp.exp2(own - behind)
        if causal is not None:
            p = jnp.where(causal, p, jnp.zeros_like(p))
        ps.append(p)
        c = c + jnp.broadcast_to(tail[:, LANES - 1:], c.shape)
    return ps[::-1], c


def _split_heads(q2):
    lane = lax.broadcasted_iota(jnp.int32, q2.shape, 1)
    zero = jnp.zeros_like(q2)
    return jnp.where(lane < HEAD_DIM, q2, zero), jnp.where(lane >= HEAD_DIM, q2, zero)


def _merge_heads(o0, o1):
    lane = lax.broadcasted_iota(jnp.int32, o0.shape, 1)
    return jnp.where(lane < HEAD_DIM, o0, o1)


def _attn_prompt_kernel(q_ref, kt_ref, v_ref, tri_ref, o_ref, q_scr, acc_ref, c_ref, *, tq, tk, kb):
    i = pl.program_id(2)
    tri = tri_ref[...]
    n_chunks = tq // tk
    for r in range(n_chunks):
        heads = _split_heads(q_ref[0, r * tk:(r + 1) * tk, :])
        for h in range(HEADS_PER_STEP):
            q_scr[(HEADS_PER_STEP * r + h) * tk:(HEADS_PER_STEP * r + h + 1) * tk, :] = heads[h]
    acc_ref[...] = jnp.zeros_like(acc_ref)
    c_ref[...] = jnp.zeros_like(c_ref)

    def weights(key_start, n_keys, row0, causal):
        kt = kt_ref[0, :, pl.ds(key_start, n_keys)]
        q_rows = q_scr[row0:, :]
        ps, c_new = _sb_weights(lambda j: _dot(q_rows, kt[:, j * tk:(j + 1) * tk]), n_keys // tk,
                                c_ref[row0:, :], tri, causal)
        c_ref[row0:, :] = c_new
        return ps

    def apply(p, key_start, n_keys, row0):
        acc_ref[row0:, :] += _dot(p, v_ref[0, pl.ds(key_start, n_keys), :])

    q0 = pl.multiple_of(i * tq, tq)
    for d in reversed(range(n_chunks)):
        row0 = HEADS_PER_STEP * d * tk
        rows = HEADS_PER_STEP * tq - row0
        row = lax.broadcasted_iota(jnp.int32, (rows, tk), 0)
        col = lax.broadcasted_iota(jnp.int32, (rows, tk), 1)
        causal = col < jnp.where(row < HEADS_PER_STEP * tk, row & (tk - 1), tk)
        (p,) = weights(q0 + d * tk, tk, row0, causal)
        apply(p, q0 + d * tk, tk, row0)

    def body(j, carry):
        start = pl.multiple_of(q0 - (j + 1) * kb, kb)
        apply(jnp.concatenate(weights(start, kb, 0, None), axis=1), start, kb, 0)
        return carry

    lax.fori_loop(0, i * (tq // kb), body, 0)
    for r in range(n_chunks):
        base = HEADS_PER_STEP * r * tk
        o_ref[0, r * tk:(r + 1) * tk, :] = _merge_heads(
            acc_ref[base:base + tk, :], acc_ref[base + tk:base + 2 * tk, :]).astype(o_ref.dtype)


def _tri_matrix(n):
    j = lax.broadcasted_iota(jnp.int32, (n, n), 0)
    k = lax.broadcasted_iota(jnp.int32, (n, n), 1)
    return ((j > k) | (k == n - 1)).astype(BF16)


def _attn_prompt(q, kt, vb):
    b, t, d = q.shape
    tq = min(Q_TILE, t)
    tk = min(K_TILE, tq)
    kb = min(K_BLOCK, tq)
    assert t % tq == 0 and tq % tk == 0 and tq % kb == 0 and kb % tk == 0
    return pl.pallas_call(
        functools.partial(_attn_prompt_kernel, tq=tq, tk=tk, kb=kb),
        grid=(b, d // PAIR, t // tq),
        in_specs=[
            pl.BlockSpec((1, tq, PAIR), lambda bi, hp, qi: (bi, qi, hp)),
            pl.BlockSpec((1, PAIR, t), lambda bi, hp, qi: (bi, hp, 0)),
            pl.BlockSpec((1, t, PAIR), lambda bi, hp, qi: (bi, 0, hp)),
            _const_spec((tk, tk)),
        ],
        out_specs=pl.BlockSpec((1, tq, PAIR), lambda bi, hp, qi: (bi, qi, hp)),
        out_shape=jax.ShapeDtypeStruct((b, t, d), BF16),
        scratch_shapes=[pltpu.VMEM((HEADS_PER_STEP * tq, PAIR), BF16),
                        pltpu.VMEM((HEADS_PER_STEP * tq, PAIR), F32),
                        pltpu.VMEM((HEADS_PER_STEP * tq, LANES), F32)],
        compiler_params=pltpu.CompilerParams(
            dimension_semantics=("arbitrary", "arbitrary", "arbitrary"), vmem_limit_bytes=VMEM_LIMIT),
        name="attn_prompt",
    )(q, kt, vb, _tri_matrix(tk))


def _attn_sample_kernel(q_ref, kn_ref, vn_ref, pk_ref, pv_ref, tri_new_ref, tri_ref, o_ref):
    tn = q_ref.shape[1]
    past = pk_ref.shape[1]
    pad = jnp.zeros((NEW_KEY_PAD - tn, PAIR), BF16)
    row = lax.broadcasted_iota(jnp.int32, (HEADS_PER_STEP * tn, NEW_KEY_PAD), 0)
    col = lax.broadcasted_iota(jnp.int32, (HEADS_PER_STEP * tn, NEW_KEY_PAD), 1)
    causal = col < jnp.where(row < tn, row, row - tn)
    for hp in range(N_HEADS // HEADS_PER_STEP):
        lanes = slice(hp * PAIR, (hp + 1) * PAIR)
        heads = range(HEADS_PER_STEP * hp, HEADS_PER_STEP * (hp + 1))
        q_rows = jnp.concatenate(_split_heads(q_ref[0, :, lanes]), axis=0)
        kn = jnp.concatenate([kn_ref[0, :, lanes].astype(BF16), pad], axis=0)
        vn = jnp.concatenate([vn_ref[0, :, lanes], pad], axis=0)
        kp = jnp.concatenate([pk_ref[0, :, h, :] for h in heads], axis=1).astype(BF16)
        vp = jnp.concatenate([pv_ref[0, :, h, :] for h in heads], axis=1).astype(BF16)
        c = jnp.zeros((HEADS_PER_STEP * tn, LANES), F32)
        (p_new,), c = _sb_weights(lambda j: _dot_nt(q_rows, kn), 1, c, tri_new_ref[...], causal)
        p_past, _ = _sb_weights(lambda j: _dot_nt(q_rows, kp[j * K_TILE:(j + 1) * K_TILE]), past // K_TILE,
                                c, tri_ref[...], None)
        o = _dot(p_new, vn) + _dot(jnp.concatenate(p_past, axis=1), vp)
        o_ref[0, :, lanes] = _merge_heads(o[:tn], o[tn:]).astype(o_ref.dtype)


def _attn_sample(q, k_new, vb_new, past_k, past_v):
    b, tn, d = q.shape
    past = past_k.shape[1]
    assert HEADS_PER_STEP == 2 and past % K_TILE == 0 and tn <= NEW_KEY_PAD and tn % 16 == 0
    assert past_k.shape[2:] == (N_HEADS, HEAD_DIM)
    new = pl.BlockSpec((1, tn, d), lambda bi: (bi, 0, 0))
    cache = pl.BlockSpec((1, past, N_HEADS, HEAD_DIM), lambda bi: (bi, 0, 0, 0))
    return pl.pallas_call(
        _attn_sample_kernel,
        grid=(b,),
        in_specs=[new, new, new, cache, cache,
                  _const_spec((NEW_KEY_PAD, NEW_KEY_PAD)), _const_spec((K_TILE, K_TILE))],
        out_specs=new,
        out_shape=jax.ShapeDtypeStruct((b, tn, d), BF16),
        compiler_params=pltpu.CompilerParams(
            dimension_semantics=("arbitrary",), vmem_limit_bytes=VMEM_LIMIT),
        name="attn_sample",
    )(q, k_new, vb_new, past_k, past_v, _tri_matrix(NEW_KEY_PAD), _tri_matrix(K_TILE))


def _out_kernel(o_ref, sz_ref, x1_ref, w_ref, g_ref, y_ref):
    gated = (o_ref[0].astype(F32) * sz_ref[0].astype(F32)).astype(BF16)
    h = _dot(gated, w_ref[...])
    y_ref[0] = x1_ref[0] + h * _inv_rms(h) * g_ref[...]


def _out_proj(o, sz, x1, w, g):
    b, t, d = x1.shape
    tm = ROW_TILE if t % ROW_TILE == 0 else t
    row = pl.BlockSpec((1, tm, d), lambda i, j: (i, j, 0))
    return pl.pallas_call(
        _out_kernel,
        grid=(b, t // tm),
        in_specs=[row, row, row, _const_spec((d, d)), _const_spec((1, d))],
        out_specs=row,
        out_shape=jax.ShapeDtypeStruct((b, t, d), F32),
        compiler_params=pltpu.CompilerParams(
            dimension_semantics=("arbitrary", "arbitrary"), vmem_limit_bytes=VMEM_LIMIT),
        name="out_proj",
    )(o, sz, x1, w, g)


def _trunk(x, conv_state, past_k, past_v, w):
    b, t, d = x.shape
    outs = _trunk_a(x, conv_state, *w["a"], emit_kt=past_k is None)
    x1, cso, k, v, q, sz, vb = outs[:7]
    if past_k is None:
        o = _attn_prompt(q, outs[7], vb)
    else:
        o = _attn_sample(q, k, vb, past_k, past_v)
    y = _out_proj(o, sz, x1, *w["out"])
    return (y, cso[None], k.reshape(b, t, N_HEADS, HEAD_DIM), v.reshape(b, t, N_HEADS, HEAD_DIM))


def kernel(x_prompt, x_sample, state_conv, cache_k, cache_v, a_norm_pre, a_w_in, a_conv_w, a_w_out, a_norm_post,
           kv_norm, w_k, w_v, b_norm_pre, b_w_in, b_w_out, b_norm_post):
    assert a_w_in.shape[0] == 1 and b_w_in.shape[0] == 1, "one conv layer followed by one attention layer"
    d = x_prompt.shape[-1]
    assert d == N_HEADS * HEAD_DIM
    vec = lambda g: g.reshape(1, d).astype(F32)
    w = {
        "a": (vec(a_norm_pre[0]), a_w_in[0].astype(BF16), a_conv_w[0].astype(F32), a_w_out[0].astype(BF16),
              vec(a_norm_post[0]), vec(kv_norm), w_k.astype(BF16), w_v.astype(BF16),
              vec(b_norm_pre[0]), b_w_in[0].astype(BF16)),
        "out": (b_w_out[0].astype(BF16), vec(b_norm_post[0])),
    }
    conv0 = jnp.zeros((x_prompt.shape[0], CONV_WIDTH - 1, d), x_prompt.dtype)
    y_p, cs_p, k_p, v_p = _trunk(x_prompt, conv0, None, None, w)
    y_s, cs_s, k_s, v_s = _trunk(x_sample, state_conv[0], cache_k, cache_v, w)
    return (y_p, y_s, cs_p, cs_s, k_p, v_p, k_s, v_s)
```

```python
import functools
import math

import jax
import jax.numpy as jnp
from jax import lax
from jax.experimental import pallas as pl
from jax.experimental.pallas import tpu as pltpu

N_HEADS = 16
HEAD_DIM = 64
HEADS_PER_STEP = 2
PAIR = HEADS_PER_STEP * HEAD_DIM
LANES = 128
EPS = 1e-6
CONV_WIDTH = 3
CONV_PAD = 8
LOG2E = math.log2(math.e)
Q_SCALE = (HEAD_DIM ** -0.5) * LOG2E
ROW_TILE = 512
Q_TILE = 2048
K_TILE = 256
K_BLOCK = 2048
NEW_KEY_PAD = 128
VMEM_LIMIT = 56 * 1024 * 1024

F32 = jnp.float32
BF16 = jnp.bfloat16


def _dot(a, b):
    return jnp.dot(a, b, preferred_element_type=F32)


def _dot_nt(a, b):
    return lax.dot_general(a, b, (((1,), (1,)), ((), ())), preferred_element_type=F32)


def _inv_rms(x):
    return lax.rsqrt(jnp.mean(x * x, axis=-1, keepdims=True) + EPS)


def _const_spec(shape):
    nd = len(shape)
    return pl.BlockSpec(shape, lambda *_: (0,) * nd, pipeline_mode=pl.Buffered(1))


def _trunk_a_kernel(x_ref, cs_ref, gpre_ref, win_ref, cw_ref, wout_ref, gpost_ref,
                    gkv_ref, wk_ref, wv_ref, gb_ref, bwin_ref,
                    x1_ref, cso_ref, k_ref, v_ref, q_ref, sz_ref, vb_ref, *rest, tm, emit_kt):
    if emit_kt:
        kt_ref, u_scr = rest
    else:
        (u_scr,) = rest
    d = x_ref.shape[-1]
    t = pl.program_id(1)

    @pl.when(t == 0)
    def _():
        u_scr[CONV_PAD - (CONV_WIDTH - 1):CONV_PAD, :] = cs_ref[0]

    x = x_ref[0]
    xn = (x * _inv_rms(x) * gpre_ref[...]).astype(BF16)
    c_gate = _dot(xn, win_ref[:, d:2 * d])
    h_in = _dot(xn, win_ref[:, 2 * d:3 * d])
    u = c_gate * h_in
    u_scr[CONV_PAD:CONV_PAD + tm, :] = u
    conv = cw_ref[CONV_WIDTH - 1:CONV_WIDTH, :] * u
    for i in range(CONV_WIDTH - 1):
        off = CONV_PAD - (CONV_WIDTH - 1) + i
        conv = conv + cw_ref[i:i + 1, :] * u_scr[off:off + tm, :]
    tail = u_scr[tm + CONV_PAD - (CONV_WIDTH - 1):tm + CONV_PAD, :]
    cso_ref[0] = tail
    u_scr[CONV_PAD - (CONV_WIDTH - 1):CONV_PAD, :] = tail

    b_gate = _dot(xn, win_ref[:, 0:d])
    z = _dot(xn, win_ref[:, 3 * d:4 * d])
    y = (z / (1.0 + jnp.exp(-z))) * b_gate * conv
    h = _dot(y.astype(BF16), wout_ref[...])
    x1 = x + h * _inv_rms(h) * gpost_ref[...]
    x1_ref[0] = x1

    xs = x1 * _inv_rms(x1)
    kv_in = (xs * gkv_ref[...]).astype(BF16)
    qn = (xs * gb_ref[...]).astype(BF16)
    k = _dot(kv_in, wk_ref[...])
    v = _dot(kv_in, wv_ref[...])
    k_ref[0] = k
    v_ref[0] = v
    vb_ref[0] = v.astype(BF16)
    if emit_kt:
        kt_ref[0] = k.T.astype(BF16)
    q = _dot(qn, bwin_ref[:, 0:d])
    q_ref[0] = (q * Q_SCALE).astype(BF16)
    zg = _dot(qn, bwin_ref[:, d:2 * d])
    sz_ref[0] = (zg / (1.0 + jnp.exp(-zg))).astype(BF16)


def _trunk_a(x, conv_state, gpre, win, cw, wout, gpost, gkv, wk, wv, gb, bwin, *, emit_kt):
    b, t, d = x.shape
    tm = ROW_TILE if t % ROW_TILE == 0 else t
    nt = t // tm
    row = lambda shape_last: pl.BlockSpec((1, tm, shape_last), lambda i, j: (i, j, 0))
    in_specs = [
        row(d),
        pl.BlockSpec((1, CONV_WIDTH - 1, d), lambda i, j: (i, 0, 0)),
        _const_spec((1, d)), _const_spec((d, 4 * d)), _const_spec((CONV_WIDTH, d)), _const_spec((d, d)),
        _const_spec((1, d)), _const_spec((1, d)), _const_spec((d, d)), _const_spec((d, d)),
        _const_spec((1, d)), _const_spec((d, 2 * d)),
    ]
    out_shape = [
        jax.ShapeDtypeStruct((b, t, d), F32),
        jax.ShapeDtypeStruct((b, CONV_WIDTH - 1, d), F32),
        jax.ShapeDtypeStruct((b, t, d), F32),
        jax.ShapeDtypeStruct((b, t, d), F32),
        jax.ShapeDtypeStruct((b, t, d), BF16),
        jax.ShapeDtypeStruct((b, t, d), BF16),
        jax.ShapeDtypeStruct((b, t, d), BF16),
    ]
    out_specs = [
        row(d),
        pl.BlockSpec((1, CONV_WIDTH - 1, d), lambda i, j: (i, 0, 0)),
        row(d), row(d), row(d), row(d), row(d),
    ]
    if emit_kt:
        out_shape.append(jax.ShapeDtypeStruct((b, d, t), BF16))
        out_specs.append(pl.BlockSpec((1, d, tm), lambda i, j: (i, 0, j)))
    return pl.pallas_call(
        functools.partial(_trunk_a_kernel, tm=tm, emit_kt=emit_kt),
        grid=(b, nt),
        in_specs=in_specs,
        out_specs=out_specs,
        out_shape=out_shape,
        scratch_shapes=[pltpu.VMEM((CONV_PAD + tm, d), F32)],
        compiler_params=pltpu.CompilerParams(
            dimension_semantics=("arbitrary", "arbitrary"), vmem_limit_bytes=VMEM_LIMIT),
        name="trunk_a_kt" if emit_kt else "trunk_a",
    )(x, conv_state, gpre, win, cw, wout, gpost, gkv, wk, wv, gb, bwin)


def _neg_abs(x):
    bits = lax.bitcast_convert_type(x, jnp.uint16) | jnp.uint16(0x8000)
    return lax.bitcast_convert_type(bits, BF16)


def _sb_weights(s_chunk, n_chunks, c, tri, causal):
    chunk = tri.shape[0]
    last_lane = lax.broadcasted_iota(jnp.int32, c.shape, 1) == LANES - 1
    ps = []
    for j in reversed(range(n_chunks)):
        sb = s_chunk(j).astype(BF16)
        e = jnp.exp2(_neg_abs(sb))
        t = 1.0 + e
        r = e - (t - 1.0)
        sp_j = jnp.maximum(sb, jnp.zeros_like(sb)) + (jnp.log(t) + r) * LOG2E
        if causal is not None:
            sp_j = jnp.where(causal, sp_j, jnp.zeros_like(sp_j))
        own = sb - sp_j
        later = _dot(sp_j, tri)
        tail = later[:, chunk - LANES:]
        parts = [later[:, t * LANES:(t + 1) * LANES] + c for t in range(chunk // LANES - 1)]
        parts.append(jnp.where(last_lane, c, tail + c))
        behind = jnp.concatenate(parts, axis=1).astype(BF16)
        p = jnp.exp2(own - behind)
        if causal is not None:
            p = jnp.where(causal, p, jnp.zeros_like(p))
        ps.append(p)
        c = c + jnp.broadcast_to(tail[:, LANES - 1:], c.shape)
    return ps[::-1], c


def _split_heads(q2):
    lane = lax.broadcasted_iota(jnp.int32, q2.shape, 1)
    zero = jnp.zeros_like(q2)
    return jnp.where(lane < HEAD_DIM, q2, zero), jnp.where(lane >= HEAD_DIM, q2, zero)


def _merge_heads(o0, o1):
    lane = lax.broadcasted_iota(jnp.int32, o0.shape, 1)
    return jnp.where(lane < HEAD_DIM, o0, o1)


def _attn_prompt_kernel(q_ref, kt_ref, v_ref, sz_ref, tri_ref, o_ref, q_scr, acc_ref, c_ref, *, tq, tk, kb):
    i = pl.program_id(2)
    tri = tri_ref[...]
    n_chunks = tq // tk
    for r in range(n_chunks):
        heads = _split_heads(q_ref[0, r * tk:(r + 1) * tk, :])
        for h in range(HEADS_PER_STEP):
            q_scr[(HEADS_PER_STEP * r + h) * tk:(HEADS_PER_STEP * r + h + 1) * tk, :] = heads[h]
    acc_ref[...] = jnp.zeros_like(acc_ref)
    c_ref[...] = jnp.zeros_like(c_ref)

    def weights(key_start, n_keys, row0, causal):
        kt = kt_ref[0, :, pl.ds(key_start, n_keys)]
        q_rows = q_scr[row0:, :]
        ps, c_new = _sb_weights(lambda j: _dot(q_rows, kt[:, j * tk:(j + 1) * tk]), n_keys // tk,
                                c_ref[row0:, :], tri, causal)
        c_ref[row0:, :] = c_new
        return ps

    def apply(p, key_start, n_keys, row0):
        acc_ref[row0:, :] += _dot(p, v_ref[0, pl.ds(key_start, n_keys), :])

    q0 = pl.multiple_of(i * tq, tq)
    for d in reversed(range(n_chunks)):
        row0 = HEADS_PER_STEP * d * tk
        rows = HEADS_PER_STEP * tq - row0
        row = lax.broadcasted_iota(jnp.int32, (rows, tk), 0)
        col = lax.broadcasted_iota(jnp.int32, (rows, tk), 1)
        causal = col < jnp.where(row < HEADS_PER_STEP * tk, row & (tk - 1), tk)
        (p,) = weights(q0 + d * tk, tk, row0, causal)
        apply(p, q0 + d * tk, tk, row0)

    def body(j, carry):
        start = pl.multiple_of(q0 - (j + 1) * kb, kb)
        apply(jnp.concatenate(weights(start, kb, 0, None), axis=1), start, kb, 0)
        return carry

    lax.fori_loop(0, i * (tq // kb), body, 0)
    for r in range(n_chunks):
        base = HEADS_PER_STEP * r * tk
        o = _merge_heads(acc_ref[base:base + tk, :], acc_ref[base + tk:base + 2 * tk, :])
        o_ref[0, r * tk:(r + 1) * tk, :] = (o * sz_ref[0, r * tk:(r + 1) * tk, :].astype(F32)).astype(o_ref.dtype)


def _tri_matrix(n):
    j = lax.broadcasted_iota(jnp.int32, (n, n), 0)
    k = lax.broadcasted_iota(jnp.int32, (n, n), 1)
    return ((j > k) | (k == n - 1)).astype(BF16)


def _attn_prompt(q, kt, vb, sz):
    b, t, d = q.shape
    tq = min(Q_TILE, t)
    tk = min(K_TILE, tq)
    kb = min(K_BLOCK, tq)
    assert t % tq == 0 and tq % tk == 0 and tq % kb == 0 and kb % tk == 0
    return pl.pallas_call(
        functools.partial(_attn_prompt_kernel, tq=tq, tk=tk, kb=kb),
        grid=(b, d // PAIR, t // tq),
        in_specs=[
            pl.BlockSpec((1, tq, PAIR), lambda bi, hp, qi: (bi, qi, hp)),
            pl.BlockSpec((1, PAIR, t), lambda bi, hp, qi: (bi, hp, 0)),
            pl.BlockSpec((1, t, PAIR), lambda bi, hp, qi: (bi, 0, hp)),
            pl.BlockSpec((1, tq, PAIR), lambda bi, hp, qi: (bi, qi, hp)),
            _const_spec((tk, tk)),
        ],
        out_specs=pl.BlockSpec((1, tq, PAIR), lambda bi, hp, qi: (bi, qi, hp)),
        out_shape=jax.ShapeDtypeStruct((b, t, d), BF16),
        scratch_shapes=[pltpu.VMEM((HEADS_PER_STEP * tq, PAIR), BF16),
                        pltpu.VMEM((HEADS_PER_STEP * tq, PAIR), F32),
                        pltpu.VMEM((HEADS_PER_STEP * tq, LANES), F32)],
        compiler_params=pltpu.CompilerParams(
            dimension_semantics=("arbitrary", "arbitrary", "arbitrary"), vmem_limit_bytes=VMEM_LIMIT),
        name="attn_prompt",
    )(q, kt, vb, sz, _tri_matrix(tk))


def _attn_sample_kernel(q_ref, kn_ref, vn_ref, sz_ref, pk_ref, pv_ref, tri_new_ref, tri_ref, o_ref):
    tn = q_ref.shape[1]
    q_rows = jnp.concatenate(_split_heads(q_ref[0]), axis=0)
    pad = jnp.zeros((NEW_KEY_PAD - tn, PAIR), BF16)
    kn = jnp.concatenate([kn_ref[0].astype(BF16), pad], axis=0)
    vn = jnp.concatenate([vn_ref[0], pad], axis=0)
    row = lax.broadcasted_iota(jnp.int32, (HEADS_PER_STEP * tn, NEW_KEY_PAD), 0)
    col = lax.broadcasted_iota(jnp.int32, (HEADS_PER_STEP * tn, NEW_KEY_PAD), 1)
    causal = col < jnp.where(row < tn, row, row - tn)
    c = jnp.zeros((HEADS_PER_STEP * tn, LANES), F32)
    (p_new,), c = _sb_weights(lambda j: _dot_nt(q_rows, kn), 1, c, tri_new_ref[...], causal)
    kp = pk_ref[0].astype(BF16)
    p_past, _ = _sb_weights(lambda j: _dot_nt(q_rows, kp[j * K_TILE:(j + 1) * K_TILE]), kp.shape[0] // K_TILE,
                            c, tri_ref[...], None)
    o = _dot(p_new, vn) + _dot(jnp.concatenate(p_past, axis=1), pv_ref[0].astype(BF16))
    o_ref[0] = (_merge_heads(o[:tn], o[tn:]) * sz_ref[0].astype(F32)).astype(o_ref.dtype)


def _attn_sample(q, k_new, vb_new, sz, past_k, past_v):
    b, tn, d = q.shape
    past = past_k.shape[1]
    assert HEADS_PER_STEP == 2 and past % K_TILE == 0 and tn <= NEW_KEY_PAD and tn % 16 == 0
    blk = lambda rows: pl.BlockSpec((1, rows, PAIR), lambda bi, hp: (bi, 0, hp))
    return pl.pallas_call(
        _attn_sample_kernel,
        grid=(b, d // PAIR),
        in_specs=[blk(tn), blk(tn), blk(tn), blk(tn), blk(past), blk(past),
                  _const_spec((NEW_KEY_PAD, NEW_KEY_PAD)), _const_spec((K_TILE, K_TILE))],
        out_specs=blk(tn),
        out_shape=jax.ShapeDtypeStruct((b, tn, d), BF16),
        compiler_params=pltpu.CompilerParams(
            dimension_semantics=("arbitrary", "arbitrary"), vmem_limit_bytes=VMEM_LIMIT),
        name="attn_sample",
    )(q, k_new, vb_new, sz, past_k.reshape(b, past, d), past_v.reshape(b, past, d),
      _tri_matrix(NEW_KEY_PAD), _tri_matrix(K_TILE))


def _out_kernel(o_ref, x1_ref, w_ref, g_ref, y_ref):
    h = _dot(o_ref[0], w_ref[...])
    y_ref[0] = x1_ref[0] + h * _inv_rms(h) * g_ref[...]


def _out_proj(o, x1, w, g):
    b, t, d = x1.shape
    tm = ROW_TILE if t % ROW_TILE == 0 else t
    row = pl.BlockSpec((1, tm, d), lambda i, j: (i, j, 0))
    return pl.pallas_call(
        _out_kernel,
        grid=(b, t // tm),
        in_specs=[row, row, _const_spec((d, d)), _const_spec((1, d))],
        out_specs=row,
        out_shape=jax.ShapeDtypeStruct((b, t, d), F32),
        compiler_params=pltpu.CompilerParams(
            dimension_semantics=("arbitrary", "arbitrary"), vmem_limit_bytes=VMEM_LIMIT),
        name="out_proj",
    )(o, x1, w, g)


def _trunk(x, conv_state, past_k, past_v, w):
    b, t, d = x.shape
    outs = _trunk_a(x, conv_state, *w["a"], emit_kt=past_k is None)
    x1, cso, k, v, q, sz, vb = outs[:7]
    if past_k is None:
        o = _attn_prompt(q, outs[7], vb, sz)
    else:
        o = _attn_sample(q, k, vb, sz, past_k, past_v)
    y = _out_proj(o, x1, *w["out"])
    return (y, cso[None], k.reshape(b, t, N_HEADS, HEAD_DIM), v.reshape(b, t, N_HEADS, HEAD_DIM))


def kernel(x_prompt, x_sample, state_conv, cache_k, cache_v, a_norm_pre, a_w_in, a_conv_w, a_w_out, a_norm_post,
           kv_norm, w_k, w_v, b_norm_pre, b_w_in, b_w_out, b_norm_post):
    assert a_w_in.shape[0] == 1 and b_w_in.shape[0] == 1, "one conv layer followed by one attention layer"
    d = x_prompt.shape[-1]
    assert d == N_HEADS * HEAD_DIM
    vec = lambda g: g.reshape(1, d).astype(F32)
    w = {
        "a": (vec(a_norm_pre[0]), a_w_in[0].astype(BF16), a_conv_w[0].astype(F32), a_w_out[0].astype(BF16),
              vec(a_norm_post[0]), vec(kv_norm), w_k.astype(BF16), w_v.astype(BF16),
              vec(b_norm_pre[0]), b_w_in[0].astype(BF16)),
        "out": (b_w_out[0].astype(BF16), vec(b_norm_post[0])),
    }
    conv0 = jnp.zeros((x_prompt.shape[0], CONV_WIDTH - 1, d), x_prompt.dtype)
    y_p, cs_p, k_p, v_p = _trunk(x_prompt, conv0, None, None, w)
    y_s, cs_s, k_s, v_s = _trunk(x_sample, state_conv[0], cache_k, cache_v, w)
    return (y_p, y_s, cs_p, cs_s, k_p, v_p, k_s, v_s)
```

```python
import functools
import math

import jax
import jax.numpy as jnp
from jax import lax
from jax.experimental import pallas as pl
from jax.experimental.pallas import tpu as pltpu

N_HEADS = 16
HEAD_DIM = 64
HEADS_PER_STEP = 2
PAIR = HEADS_PER_STEP * HEAD_DIM
LANES = 128
EPS = 1e-6
CONV_WIDTH = 3
CONV_PAD = 8
LOG2E = math.log2(math.e)
Q_SCALE = (HEAD_DIM ** -0.5) * LOG2E
ROW_TILE = 512
Q_TILE = 2048
K_TILE = 256
K_BLOCK = 2048
NEW_KEY_PAD = 128
VMEM_LIMIT = 56 * 1024 * 1024

F32 = jnp.float32
BF16 = jnp.bfloat16


def _dot(a, b):
    return jnp.dot(a, b, preferred_element_type=F32)


def _dot_nt(a, b):
    return lax.dot_general(a, b, (((1,), (1,)), ((), ())), preferred_element_type=F32)


def _inv_rms(x):
    return lax.rsqrt(jnp.mean(x * x, axis=-1, keepdims=True) + EPS)


def _const_spec(shape):
    nd = len(shape)
    return pl.BlockSpec(shape, lambda *_: (0,) * nd, pipeline_mode=pl.Buffered(1))


def _trunk_a_kernel(x_ref, cs_ref, gpre_ref, win_ref, cw_ref, wout_ref, gpost_ref,
                    gkv_ref, wk_ref, wv_ref, gb_ref, bwin_ref,
                    x1_ref, cso_ref, k_ref, v_ref, q_ref, sz_ref, vb_ref, *rest, tm, emit_kt):
    if emit_kt:
        kt_ref, u_scr = rest
    else:
        (u_scr,) = rest
    d = x_ref.shape[-1]
    t = pl.program_id(1)

    @pl.when(t == 0)
    def _():
        u_scr[CONV_PAD - (CONV_WIDTH - 1):CONV_PAD, :] = cs_ref[0]

    x = x_ref[0]
    xn = (x * _inv_rms(x) * gpre_ref[...]).astype(BF16)
    c_gate = _dot(xn, win_ref[:, d:2 * d])
    h_in = _dot(xn, win_ref[:, 2 * d:3 * d])
    u = c_gate * h_in
    u_scr[CONV_PAD:CONV_PAD + tm, :] = u
    conv = cw_ref[CONV_WIDTH - 1:CONV_WIDTH, :] * u
    for i in range(CONV_WIDTH - 1):
        off = CONV_PAD - (CONV_WIDTH - 1) + i
        conv = conv + cw_ref[i:i + 1, :] * u_scr[off:off + tm, :]
    tail = u_scr[tm + CONV_PAD - (CONV_WIDTH - 1):tm + CONV_PAD, :]
    cso_ref[0] = tail
    u_scr[CONV_PAD - (CONV_WIDTH - 1):CONV_PAD, :] = tail

    b_gate = _dot(xn, win_ref[:, 0:d])
    z = _dot(xn, win_ref[:, 3 * d:4 * d])
    y = (z / (1.0 + jnp.exp(-z))) * b_gate * conv
    h = _dot(y.astype(BF16), wout_ref[...])
    x1 = x + h * _inv_rms(h) * gpost_ref[...]
    x1_ref[0] = x1

    xs = x1 * _inv_rms(x1)
    kv_in = (xs * gkv_ref[...]).astype(BF16)
    qn = (xs * gb_ref[...]).astype(BF16)
    k = _dot(kv_in, wk_ref[...])
    v = _dot(kv_in, wv_ref[...])
    k_ref[0] = k
    v_ref[0] = v
    vb_ref[0] = v.astype(BF16)
    if emit_kt:
        kt_ref[0] = k.T.astype(BF16)
    q = _dot(qn, bwin_ref[:, 0:d])
    q_ref[0] = (q * Q_SCALE).astype(BF16)
    zg = _dot(qn, bwin_ref[:, d:2 * d])
    sz_ref[0] = (zg / (1.0 + jnp.exp(-zg))).astype(BF16)


def _trunk_a(x, conv_state, gpre, win, cw, wout, gpost, gkv, wk, wv, gb, bwin, *, emit_kt):
    b, t, d = x.shape
    tm = ROW_TILE if t % ROW_TILE == 0 else t
    nt = t // tm
    row = lambda shape_last: pl.BlockSpec((1, tm, shape_last), lambda i, j: (i, j, 0))
    in_specs = [
        row(d),
        pl.BlockSpec((1, CONV_WIDTH - 1, d), lambda i, j: (i, 0, 0)),
        _const_spec((1, d)), _const_spec((d, 4 * d)), _const_spec((CONV_WIDTH, d)), _const_spec((d, d)),
        _const_spec((1, d)), _const_spec((1, d)), _const_spec((d, d)), _const_spec((d, d)),
        _const_spec((1, d)), _const_spec((d, 2 * d)),
    ]
    out_shape = [
        jax.ShapeDtypeStruct((b, t, d), F32),
        jax.ShapeDtypeStruct((b, CONV_WIDTH - 1, d), F32),
        jax.ShapeDtypeStruct((b, t, d), F32),
        jax.ShapeDtypeStruct((b, t, d), F32),
        jax.ShapeDtypeStruct((b, t, d), BF16),
        jax.ShapeDtypeStruct((b, t, d), BF16),
        jax.ShapeDtypeStruct((b, t, d), BF16),
    ]
    out_specs = [
        row(d),
        pl.BlockSpec((1, CONV_WIDTH - 1, d), lambda i, j: (i, 0, 0)),
        row(d), row(d), row(d), row(d), row(d),
    ]
    if emit_kt:
        out_shape.append(jax.ShapeDtypeStruct((b, d, t), BF16))
        out_specs.append(pl.BlockSpec((1, d, tm), lambda i, j: (i, 0, j)))
    return pl.pallas_call(
        functools.partial(_trunk_a_kernel, tm=tm, emit_kt=emit_kt),
        grid=(b, nt),
        in_specs=in_specs,
        out_specs=out_specs,
        out_shape=out_shape,
        scratch_shapes=[pltpu.VMEM((CONV_PAD + tm, d), F32)],
        compiler_params=pltpu.CompilerParams(
            dimension_semantics=("arbitrary", "arbitrary"), vmem_limit_bytes=VMEM_LIMIT),
        name="trunk_a_kt" if emit_kt else "trunk_a",
    )(x, conv_state, gpre, win, cw, wout, gpost, gkv, wk, wv, gb, bwin)


def _neg_abs(x):
    bits = lax.bitcast_convert_type(x, jnp.uint16) | jnp.uint16(0x8000)
    return lax.bitcast_convert_type(bits, BF16)


def _sb_weights(s_chunk, n_chunks, c, tri, causal):
    chunk = tri.shape[0]
    last_lane = lax.broadcasted_iota(jnp.int32, c.shape, 1) == LANES - 1
    ps = []
    for j in reversed(range(n_chunks)):
        sb = s_chunk(j).astype(BF16)
        e = jnp.exp2(_neg_abs(sb))
        t = 1.0 + e
        r = e - (t - 1.0)
        sp_j = jnp.maximum(sb, jnp.zeros_like(sb)) + (jnp.log(t) + r) * LOG2E
        if causal is not None:
            sp_j = jnp.where(causal, sp_j, jnp.zeros_like(sp_j))
        own = sb - sp_j
        later = _dot(sp_j, tri)
        tail = later[:, chunk - LANES:]
        parts = [later[:, t * LANES:(t + 1) * LANES] + c for t in range(chunk // LANES - 1)]
        parts.append(jnp.where(last_lane, c, tail + c))
        behind = jnp.concatenate(parts, axis=1).astype(BF16)
        p = jnp.exp2(own - behind)
        if causal is not None:
            p = jnp.where(causal, p, jnp.zeros_like(p))
        ps.append(p)
        c = c + jnp.broadcast_to(tail[:, LANES - 1:], c.shape)
    return ps[::-1], c


def _split_heads(q2):
    lane = lax.broadcasted_iota(jnp.int32, q2.shape, 1)
    zero = jnp.zeros_like(q2)
    return jnp.where(lane < HEAD_DIM, q2, zero), jnp.where(lane >= HEAD_DIM, q2, zero)


def _merge_heads(o0, o1):
    lane = lax.broadcasted_iota(jnp.int32, o0.shape, 1)
    return jnp.where(lane < HEAD_DIM, o0, o1)


def _attn_prompt_kernel(q_ref, kt_ref, v_ref, tri_ref, o_ref, q_scr, acc_ref, c_ref, *, tq, tk, kb):
    i = pl.program_id(2)
    tri = tri_ref[...]
    n_chunks = tq // tk
    for r in range(n_chunks):
        heads = _split_heads(q_ref[0, r * tk:(r + 1) * tk, :])
        for h in range(HEADS_PER_STEP):
            q_scr[(HEADS_PER_STEP * r + h) * tk:(HEADS_PER_STEP * r + h + 1) * tk, :] = heads[h]
    acc_ref[...] = jnp.zeros_like(acc_ref)
    c_ref[...] = jnp.zeros_like(c_ref)

    def weights(key_start, n_keys, row0, causal):
        kt = kt_ref[0, :, pl.ds(key_start, n_keys)]
        q_rows = q_scr[row0:, :]
        ps, c_new = _sb_weights(lambda j: _dot(q_rows, kt[:, j * tk:(j + 1) * tk]), n_keys // tk,
                                c_ref[row0:, :], tri, causal)
        c_ref[row0:, :] = c_new
        return ps

    def apply(p, key_start, n_keys, row0):
        acc_ref[row0:, :] += _dot(p, v_ref[0, pl.ds(key_start, n_keys), :])

    q0 = pl.multiple_of(i * tq, tq)
    for d in reversed(range(n_chunks)):
        row0 = HEADS_PER_STEP * d * tk
        rows = HEADS_PER_STEP * tq - row0
        row = lax.broadcasted_iota(jnp.int32, (rows, tk), 0)
        col = lax.broadcasted_iota(jnp.int32, (rows, tk), 1)
        causal = col < jnp.where(row < HEADS_PER_STEP * tk, row & (tk - 1), tk)
        (p,) = weights(q0 + d * tk, tk, row0, causal)
        apply(p, q0 + d * tk, tk, row0)

    def body(j, carry):
        start = pl.multiple_of(q0 - (j + 1) * kb, kb)
        apply(jnp.concatenate(weights(start, kb, 0, None), axis=1), start, kb, 0)
        return carry

    lax.fori_loop(0, i * (tq // kb), body, 0)
    for r in range(n_chunks):
        base = HEADS_PER_STEP * r * tk
        o_ref[0, r * tk:(r + 1) * tk, :] = _merge_heads(
            acc_ref[base:base + tk, :], acc_ref[base + tk:base + 2 * tk, :]).astype(o_ref.dtype)


def _tri_matrix(n):
    j = lax.broadcasted_iota(jnp.int32, (n, n), 0)
    k = lax.broadcasted_iota(jnp.int32, (n, n), 1)
    return ((j > k) | (k == n - 1)).astype(BF16)


def _attn_prompt(q, kt, vb):
    b, t, d = q.shape
    tq = min(Q_TILE, t)
    tk = min(K_TILE, tq)
    kb = min(K_BLOCK, tq)
    assert t % tq == 0 and tq % tk == 0 and tq % kb == 0 and kb % tk == 0
    return pl.pallas_call(
        functools.partial(_attn_prompt_kernel, tq=tq, tk=tk, kb=kb),
        grid=(b, d // PAIR, t // tq),
        in_specs=[
            pl.BlockSpec((1, tq, PAIR), lambda bi, hp, qi: (bi, qi, hp)),
            pl.BlockSpec((1, PAIR, t), lambda bi, hp, qi: (bi, hp, 0)),
            pl.BlockSpec((1, t, PAIR), lambda bi, hp, qi: (bi, 0, hp)),
            _const_spec((tk, tk)),
        ],
        out_specs=pl.BlockSpec((1, tq, PAIR), lambda bi, hp, qi: (bi, qi, hp)),
        out_shape=jax.ShapeDtypeStruct((b, t, d), BF16),
        scratch_shapes=[pltpu.VMEM((HEADS_PER_STEP * tq, PAIR), BF16),
                        pltpu.VMEM((HEADS_PER_STEP * tq, PAIR), F32),
                        pltpu.VMEM((HEADS_PER_STEP * tq, LANES), F32)],
        compiler_params=pltpu.CompilerParams(
            dimension_semantics=("arbitrary", "arbitrary", "arbitrary"), vmem_limit_bytes=VMEM_LIMIT),
        name="attn_prompt",
    )(q, kt, vb, _tri_matrix(tk))


def _attn_sample_kernel(q_ref, kn_ref, vn_ref, pk_ref, pv_ref, tri_new_ref, tri_ref, o_ref):
    tn = q_ref.shape[1]
    q_rows = jnp.concatenate(_split_heads(q_ref[0]), axis=0)
    pad = jnp.zeros((NEW_KEY_PAD - tn, PAIR), BF16)
    kn = jnp.concatenate([kn_ref[0].astype(BF16), pad], axis=0)
    vn = jnp.concatenate([vn_ref[0], pad], axis=0)
    row = lax.broadcasted_iota(jnp.int32, (HEADS_PER_STEP * tn, NEW_KEY_PAD), 0)
    col = lax.broadcasted_iota(jnp.int32, (HEADS_PER_STEP * tn, NEW_KEY_PAD), 1)
    causal = col < jnp.where(row < tn, row, row - tn)
    c = jnp.zeros((HEADS_PER_STEP * tn, LANES), F32)
    (p_new,), c = _sb_weights(lambda j: _dot_nt(q_rows, kn), 1, c, tri_new_ref[...], causal)
    kp = pk_ref[0].astype(BF16)
    p_past, _ = _sb_weights(lambda j: _dot_nt(q_rows, kp[j * K_TILE:(j + 1) * K_TILE]), kp.shape[0] // K_TILE,
                            c, tri_ref[...], None)
    o = _dot(p_new, vn) + _dot(jnp.concatenate(p_past, axis=1), pv_ref[0].astype(BF16))
    o_ref[0] = _merge_heads(o[:tn], o[tn:]).astype(o_ref.dtype)


def _attn_sample(q, k_new, vb_new, past_k, past_v):
    b, tn, d = q.shape
    past = past_k.shape[1]
    assert HEADS_PER_STEP == 2 and past % K_TILE == 0 and tn <= NEW_KEY_PAD and tn % 16 == 0
    blk = lambda rows: pl.BlockSpec((1, rows, PAIR), lambda bi, hp: (bi, 0, hp))
    return pl.pallas_call(
        _attn_sample_kernel,
        grid=(b, d // PAIR),
        in_specs=[blk(tn), blk(tn), blk(tn), blk(past), blk(past),
                  _const_spec((NEW_KEY_PAD, NEW_KEY_PAD)), _const_spec((K_TILE, K_TILE))],
        out_specs=blk(tn),
        out_shape=jax.ShapeDtypeStruct((b, tn, d), BF16),
        compiler_params=pltpu.CompilerParams(
            dimension_semantics=("arbitrary", "arbitrary"), vmem_limit_bytes=VMEM_LIMIT),
        name="attn_sample",
    )(q, k_new, vb_new, past_k.astype(BF16).reshape(b, past, d), past_v.astype(BF16).reshape(b, past, d),
      _tri_matrix(NEW_KEY_PAD), _tri_matrix(K_TILE))


def _out_kernel(o_ref, sz_ref, x1_ref, w_ref, g_ref, y_ref):
    gated = (o_ref[0].astype(F32) * sz_ref[0].astype(F32)).astype(BF16)
    h = _dot(gated, w_ref[...])
    y_ref[0] = x1_ref[0] + h * _inv_rms(h) * g_ref[...]


def _out_proj(o, sz, x1, w, g):
    b, t, d = x1.shape
    tm = ROW_TILE if t % ROW_TILE == 0 else t
    row = pl.BlockSpec((1, tm, d), lambda i, j: (i, j, 0))
    return pl.pallas_call(
        _out_kernel,
        grid=(b, t // tm),
        in_specs=[row, row, row, _const_spec((d, d)), _const_spec((1, d))],
        out_specs=row,
        out_shape=jax.ShapeDtypeStruct((b, t, d), F32),
        compiler_params=pltpu.CompilerParams(
            dimension_semantics=("arbitrary", "arbitrary"), vmem_limit_bytes=VMEM_LIMIT),
        name="out_proj",
    )(o, sz, x1, w, g)


def _trunk(x, conv_state, past_k, past_v, w):
    b, t, d = x.shape
    outs = _trunk_a(x, conv_state, *w["a"], emit_kt=past_k is None)
    x1, cso, k, v, q, sz, vb = outs[:7]
    if past_k is None:
        o = _attn_prompt(q, outs[7], vb)
    else:
        o = _attn_sample(q, k, vb, past_k, past_v)
    y = _out_proj(o, sz, x1, *w["out"])
    return (y, cso[None], k.reshape(b, t, N_HEADS, HEAD_DIM), v.reshape(b, t, N_HEADS, HEAD_DIM))


def kernel(x_prompt, x_sample, state_conv, cache_k, cache_v, a_norm_pre, a_w_in, a_conv_w, a_w_out, a_norm_post,
           kv_norm, w_k, w_v, b_norm_pre, b_w_in, b_w_out, b_norm_post):
    assert a_w_in.shape[0] == 1 and b_w_in.shape[0] == 1, "one conv layer followed by one attention layer"
    d = x_prompt.shape[-1]
    assert d == N_HEADS * HEAD_DIM
    vec = lambda g: g.reshape(1, d).astype(F32)
    w = {
        "a": (vec(a_norm_pre[0]), a_w_in[0].astype(BF16), a_conv_w[0].astype(F32), a_w_out[0].astype(BF16),
              vec(a_norm_post[0]), vec(kv_norm), w_k.astype(BF16), w_v.astype(BF16),
              vec(b_norm_pre[0]), b_w_in[0].astype(BF16)),
        "out": (b_w_out[0].astype(BF16), vec(b_norm_post[0])),
    }
    conv0 = jnp.zeros((x_prompt.shape[0], CONV_WIDTH - 1, d), x_prompt.dtype)
    y_p, cs_p, k_p, v_p = _trunk(x_prompt, conv0, None, None, w)
    y_s, cs_s, k_s, v_s = _trunk(x_sample, state_conv[0], cache_k, cache_v, w)
    return (y_p, y_s, cs_p, cs_s, k_p, v_p, k_s, v_s)
```
